```python
import math
import jax
import jax.numpy as jnp
from jax import lax
import numpy as np

D_MODEL = 1024
BATCH = 4
SEQ = 8192
DEPTH = 1
DEC_BATCH = 32
DEC_SEQ = 4
PAST_LEN = 16384
PAGE_SIZE = 128

D_MIX = D_MODEL
D_SSM = D_MIX // 2
D_DIFF = D_MIX - D_SSM
P_SSM = 64
H_SSM = D_SSM // P_SSM
N_STATE = 128
SSM_GROUPS = 2
HEADS_PER_GROUP = H_SSM // SSM_GROUPS
CONV_W = 4
CONV_DIM = D_SSM + 2 * SSM_GROUPS * N_STATE
SSD_CHUNK = 128
DH_DIFF = 64
H_DIFF = D_DIFF // (2 * DH_DIFF)
Q_BLOCK = 128
N_BUCKETS = 32
REL_MAX_DIST = 128
N_KEYS = 128
N_EXPERTS = N_KEYS * N_KEYS
PEER_HEADS = 8
PEER_TOPK = 16
D_KEY = 256
PEER_BLOCK = 128
PLE_DIM = 256
IN_COLS = D_SSM + CONV_DIM + H_SSM + 3 * D_DIFF
SPLIT_AT = (D_SSM, D_SSM + CONV_DIM, D_SSM + CONV_DIM + H_SSM,
            D_SSM + CONV_DIM + H_SSM + D_DIFF, D_SSM + CONV_DIM + H_SSM + 2 * D_DIFF)
NEG_INF = -1e30
RMS_EPS = 1e-6

kernel_name = 'hybrid_ssd_diffattn_peer_step'


def rmsnorm(x, g):
    xf = x.astype(jnp.float32)
    y = xf * lax.rsqrt(jnp.mean(xf * xf, axis=-1, keepdims=True) + RMS_EPS)
    return (y * g.astype(jnp.float32)).astype(x.dtype)


def t5_bucket(rel):
    n = jnp.maximum(rel, 0)
    max_exact = N_BUCKETS // 2
    nf = jnp.maximum(n, 1).astype(jnp.float32)
    large = max_exact + (jnp.log(nf / max_exact) / math.log(REL_MAX_DIST / max_exact)
                         * (N_BUCKETS - max_exact)).astype(jnp.int32)
    large = jnp.minimum(large, N_BUCKETS - 1)
    return jnp.where(n < max_exact, n, large)


def segsum(a):
    t = a.shape[-1]
    cs = jnp.cumsum(a, axis=-1)
    diff = cs[..., :, None] - cs[..., None, :]
    mask = jnp.tril(jnp.ones((t, t), dtype=bool))
    return jnp.where(mask, diff, -jnp.inf)


def ssd_chunked(xdt, adt, b, c, state0, chunk):
    bsz, seqlen, nh, hp = xdt.shape
    nc = seqlen // chunk
    xc = xdt.reshape(bsz, nc, chunk, nh, hp)
    ac = jnp.transpose(adt.reshape(bsz, nc, chunk, nh), (0, 3, 1, 2))
    bc = b.reshape(bsz, nc, chunk, nh, N_STATE)
    cc = c.reshape(bsz, nc, chunk, nh, N_STATE)
    a_cs = jnp.cumsum(ac, axis=-1)
    l_intra = jnp.exp(segsum(ac))
    y_diag = jnp.einsum('bclhn,bcshn,bhcls,bcshp->bclhp', cc, bc, l_intra, xc)
    decay_states = jnp.exp(a_cs[..., -1:] - a_cs)
    states = jnp.einsum('bclhn,bhcl,bclhp->bchpn', bc, decay_states, xc)
    states = jnp.concatenate([state0[:, None], states], axis=1)
    chunk_decay = jnp.exp(segsum(jnp.pad(a_cs[..., -1], ((0, 0), (0, 0), (1, 0)))))
    new_states = jnp.einsum('bhzc,bchpn->bzhpn', chunk_decay, states)
    states_in, final = new_states[:, :-1], new_states[:, -1]
    y_off = jnp.einsum('bclhn,bchpn,bhcl->bclhp', cc, states_in, jnp.exp(a_cs))
    return (y_diag + y_off).reshape(bsz, seqlen, nh, hp), final


def causal_dwconv(xpad, w):
    ch = xpad.shape[-1]
    return lax.conv_general_dilated(xpad, w[:, None, :].astype(xpad.dtype), window_strides=(1,),
                                    padding='VALID', dimension_numbers=('NWC', 'WIO', 'NWC'),
                                    feature_group_count=ch)


def diff_attention(q, k, v, q_pos0, lam, rel_bias):
    bsz, lq = q.shape[0], q.shape[1]
    lk = k.shape[1]
    blk = math.gcd(lq, Q_BLOCK)
    nblk = lq // blk
    kpos = jnp.arange(lk, dtype=jnp.int32)
    qb = jnp.moveaxis(q.reshape(bsz, nblk, blk, H_DIFF, 2, DH_DIFF), 1, 0)
    scale = DH_DIFF ** -0.5

    def one_block(args):
        q_blk, i = args
        qpos = q_pos0 + i * blk + jnp.arange(blk, dtype=jnp.int32)
        s = jnp.einsum('bqhmd,bkhmd->bhmqk', q_blk, k).astype(jnp.float32) * scale
        rel = qpos[:, None] - kpos[None, :]
        bias = jnp.moveaxis(rel_bias[t5_bucket(rel)], -1, 0).astype(jnp.float32)
        s = jnp.where(rel >= 0, s + bias[None, :, None], NEG_INF)
        pm = jax.nn.softmax(s, axis=-1)
        w = pm[:, :, 0] - lam * pm[:, :, 1]
        return jnp.einsum('bhqk,bkhe->bqhe', w.astype(v.dtype), v)

    o = lax.map(one_block, (qb, jnp.arange(nblk, dtype=jnp.int32)))
    return jnp.moveaxis(o, 0, 1).reshape(bsz, lq, H_DIFF, 2 * DH_DIFF)


def peer(xn, w_query, sub_keys, expert_u, expert_v):
    t = xn.shape[0]
    blk = math.gcd(t, PEER_BLOCK)
    xb = xn.reshape(t // blk, blk, D_MODEL)

    def one(xt):
        q = (xt @ w_query).reshape(blk, PEER_HEADS, 2, D_KEY // 2)
        s = jnp.einsum('thcd,cnd->thcn', q, sub_keys).astype(jnp.float32)
        s1, i1 = lax.top_k(s[:, :, 0], PEER_TOPK)
        s2, i2 = lax.top_k(s[:, :, 1], PEER_TOPK)
        cand = (s1[..., :, None] + s2[..., None, :]).reshape(blk, PEER_HEADS, PEER_TOPK * PEER_TOPK)
        cidx = (i1[..., :, None] * N_KEYS + i2[..., None, :]).reshape(blk, PEER_HEADS, PEER_TOPK * PEER_TOPK)
        sc, pos = lax.top_k(cand, PEER_TOPK)
        idx = jnp.take_along_axis(cidx, pos, axis=-1)
        g = jax.nn.softmax(sc, axis=-1)
        u = expert_u[idx]
        act = jax.nn.gelu(jnp.einsum('thkd,td->thk', u, xt).astype(jnp.float32), approximate=False)
        vv = expert_v[idx]
        return jnp.einsum('thk,thkd->td', (g * act).astype(vv.dtype), vv)

    return lax.map(one, xb).reshape(t, D_MODEL)


def decoder_layer(h, p_l, conv_buf, ssm0, past_k, past_v, pos0, lam_init,
                  g_mix, w_in, conv_w, conv_b, dt_bias, a_log, d_skip, g_ssm_norm,
                  lam_q1, lam_k1, lam_q2, lam_k2, g_diff_norm, rel_bias, w_out,
                  g_ffn, w_query, sub_keys, expert_u, expert_v, w_ple, w_ple_gate, g_ple):
    bsz, seqlen = h.shape[0], h.shape[1]
    a = rmsnorm(h, g_mix)
    proj = a @ w_in
    z, xbc, dt_raw, q, k, v = jnp.split(proj, SPLIT_AT, axis=-1)

    xpad = jnp.concatenate([conv_buf.astype(xbc.dtype), xbc], axis=1)
    conv_new = xpad[:, -(CONV_W - 1):]
    xc = jax.nn.silu(causal_dwconv(xpad, conv_w) + conv_b)
    xs, bm, cm = jnp.split(xc, (D_SSM, D_SSM + SSM_GROUPS * N_STATE), axis=-1)
    xs = xs.reshape(bsz, seqlen, H_SSM, P_SSM).astype(jnp.float32)
    bh = jnp.repeat(bm.reshape(bsz, seqlen, SSM_GROUPS, N_STATE), HEADS_PER_GROUP, axis=2).astype(jnp.float32)
    ch = jnp.repeat(cm.reshape(bsz, seqlen, SSM_GROUPS, N_STATE), HEADS_PER_GROUP, axis=2).astype(jnp.float32)
    dt = jax.nn.softplus(dt_raw.astype(jnp.float32) + dt_bias.astype(jnp.float32))
    adt = -jnp.exp(a_log.astype(jnp.float32)) * dt
    y, ssm_final = ssd_chunked(xs * dt[..., None], adt, bh, ch, ssm0.astype(jnp.float32),
                               math.gcd(seqlen, SSD_CHUNK))
    y = y + d_skip.astype(jnp.float32)[:, None] * xs
    y = y.reshape(bsz, seqlen, D_SSM) * jax.nn.silu(z.astype(jnp.float32))
    yg = y.reshape(bsz, seqlen, SSM_GROUPS, D_SSM // SSM_GROUPS)
    yg = yg * lax.rsqrt(jnp.mean(yg * yg, axis=-1, keepdims=True) + RMS_EPS)
    y_ssm = (yg.reshape(bsz, seqlen, D_SSM) * g_ssm_norm.astype(jnp.float32)).astype(h.dtype)

    q = q.reshape(bsz, seqlen, H_DIFF, 2, DH_DIFF)
    k = k.reshape(bsz, seqlen, H_DIFF, 2, DH_DIFF)
    v = v.reshape(bsz, seqlen, H_DIFF, 2 * DH_DIFF)
    if past_k is None:
        k_all, v_all = k, v
    else:
        k_all = jnp.concatenate([past_k.astype(k.dtype), k], axis=1)
        v_all = jnp.concatenate([past_v.astype(v.dtype), v], axis=1)
    lam = (jnp.exp(jnp.sum(lam_q1.astype(jnp.float32) * lam_k1.astype(jnp.float32)))
           - jnp.exp(jnp.sum(lam_q2.astype(jnp.float32) * lam_k2.astype(jnp.float32))) + lam_init)
    o = diff_attention(q, k_all, v_all, pos0, lam, rel_bias)
    o = rmsnorm(o, g_diff_norm) * (1.0 - lam_init)
    y_diff = o.reshape(bsz, seqlen, D_DIFF).astype(h.dtype)

    h = h + jnp.concatenate([y_ssm, y_diff], axis=-1) @ w_out
    m = rmsnorm(h, g_ffn)
    h = h + peer(m.reshape(bsz * seqlen, D_MODEL), w_query, sub_keys, expert_u, expert_v).reshape(bsz, seqlen, D_MODEL)
    h = h + (p_l @ w_ple) * jax.nn.sigmoid(rmsnorm(h, g_ple) @ w_ple_gate)
    return h, k, v, ssm_final.astype(ssm0.dtype), conv_new


def setup_inputs(seed: int = 0) -> dict:
    key = jax.random.key(seed)
    ks = jax.random.split(key, 40)
    f32 = jnp.float32
    n_pages = PAST_LEN // PAGE_SIZE
    n_pool = (DEC_BATCH * n_pages * 5) // 4

    def nrm(k, shape, s):
        return jax.random.normal(k, shape, f32) * s

    dt0 = jnp.exp(jax.random.uniform(ks[14], (DEPTH, H_SSM), f32) * (math.log(0.1) - math.log(0.001)) + math.log(0.001))
    return {
        'x_prompt': nrm(ks[0], (BATCH, SEQ, D_MODEL), 1.0),
        'x_sample': nrm(ks[1], (DEC_BATCH, DEC_SEQ, D_MODEL), 1.0),
        'cache_k': nrm(ks[2], (DEPTH, n_pool, PAGE_SIZE, H_DIFF, 2, DH_DIFF), 1.0),
        'cache_v': nrm(ks[3], (DEPTH, n_pool, PAGE_SIZE, H_DIFF, 2 * DH_DIFF), 1.0),
        'state_ssm': nrm(ks[4], (DEPTH, DEC_BATCH, H_SSM, P_SSM, N_STATE), 0.1),
        'state_conv': nrm(ks[5], (DEPTH, DEC_BATCH, CONV_W - 1, CONV_DIM), 1.0),
        'page_table': jax.random.permutation(ks[6], n_pool)[:DEC_BATCH * n_pages].reshape(DEC_BATCH, n_pages).astype(jnp.int32),
        'p_prompt': nrm(ks[7], (DEPTH, BATCH, SEQ, PLE_DIM), 1.0),
        'p_sample': nrm(ks[8], (DEPTH, DEC_BATCH, DEC_SEQ, PLE_DIM), 1.0),
        'g_mix': 1.0 + nrm(ks[9], (DEPTH, D_MODEL), 0.02),
        'w_in': nrm(ks[10], (DEPTH, D_MODEL, IN_COLS), D_MODEL ** -0.5),
        'conv_w': nrm(ks[11], (DEPTH, CONV_W, CONV_DIM), CONV_W ** -0.5),
        'conv_b': nrm(ks[12], (DEPTH, CONV_DIM), 0.02),
        'dt_bias': dt0 + jnp.log(-jnp.expm1(-dt0)),
        'a_log': jnp.log(jax.random.uniform(ks[15], (DEPTH, H_SSM), f32, 1.0, 16.0)),
        'd_skip': 1.0 + nrm(ks[16], (DEPTH, H_SSM), 0.02),
        'g_ssm_norm': 1.0 + nrm(ks[17], (DEPTH, D_SSM), 0.02),
        'lam_q1': nrm(ks[18], (DEPTH, DH_DIFF), 0.1),
        'lam_k1': nrm(ks[19], (DEPTH, DH_DIFF), 0.1),
        'lam_q2': nrm(ks[20], (DEPTH, DH_DIFF), 0.1),
        'lam_k2': nrm(ks[21], (DEPTH, DH_DIFF), 0.1),
        'g_diff_norm': 1.0 + nrm(ks[22], (DEPTH, 2 * DH_DIFF), 0.02),
        'rel_bias': nrm(ks[23], (N_BUCKETS, H_DIFF), 0.1),
        'w_out': nrm(ks[24], (DEPTH, D_MIX, D_MODEL), D_MIX ** -0.5),
        'g_ffn': 1.0 + nrm(ks[25], (DEPTH, D_MODEL), 0.02),
        'w_query': nrm(ks[26], (DEPTH, D_MODEL, PEER_HEADS * D_KEY), D_MODEL ** -0.5),
        'sub_keys': nrm(ks[27], (DEPTH, 2, N_KEYS, D_KEY // 2), (D_KEY // 2) ** -0.5),
        'expert_u': nrm(ks[28], (DEPTH, N_EXPERTS, D_MODEL), D_MODEL ** -0.5),
        'expert_v': nrm(ks[29], (DEPTH, N_EXPERTS, D_MODEL), 0.5),
        'w_ple': nrm(ks[30], (DEPTH, PLE_DIM, D_MODEL), PLE_DIM ** -0.5),
        'w_ple_gate': nrm(ks[31], (DEPTH, D_MODEL, D_MODEL), D_MODEL ** -0.5),
        'g_ple': 1.0 + nrm(ks[32], (DEPTH, D_MODEL), 0.02),
        'g_final': 1.0 + nrm(ks[33], (D_MODEL,), 0.02),
    }


def reference(x_prompt, x_sample, cache_k, cache_v, state_ssm, state_conv, page_table, p_prompt, p_sample,
              g_mix, w_in, conv_w, conv_b, dt_bias, a_log, d_skip, g_ssm_norm,
              lam_q1, lam_k1, lam_q2, lam_k2, g_diff_norm, rel_bias, w_out,
              g_ffn, w_query, sub_keys, expert_u, expert_v, w_ple, w_ple_gate, g_ple, g_final):
    dec_b = x_sample.shape[0]
    past_len = page_table.shape[1] * cache_k.shape[2]
    hp, hs = x_prompt, x_sample
    kp, vp, sp, cp = [], [], [], []
    ks_, vs_, ss_, cs_ = [], [], [], []
    for l in range(DEPTH):
        lam_init = 0.8 - 0.6 * math.exp(-0.3 * l)
        wl = dict(g_mix=g_mix[l], w_in=w_in[l], conv_w=conv_w[l], conv_b=conv_b[l], dt_bias=dt_bias[l],
                  a_log=a_log[l], d_skip=d_skip[l], g_ssm_norm=g_ssm_norm[l],
                  lam_q1=lam_q1[l], lam_k1=lam_k1[l], lam_q2=lam_q2[l], lam_k2=lam_k2[l],
                  g_diff_norm=g_diff_norm[l], rel_bias=rel_bias, w_out=w_out[l], g_ffn=g_ffn[l],
                  w_query=w_query[l], sub_keys=sub_keys[l], expert_u=expert_u[l], expert_v=expert_v[l],
                  w_ple=w_ple[l], w_ple_gate=w_ple_gate[l], g_ple=g_ple[l])
        conv0 = jnp.zeros((hp.shape[0], CONV_W - 1, CONV_DIM), hp.dtype)
        ssm0 = jnp.zeros((hp.shape[0], H_SSM, P_SSM, N_STATE), jnp.float32)
        hp, k_n, v_n, s_n, c_n = decoder_layer(hp, p_prompt[l], conv0, ssm0, None, None, 0, lam_init, **wl)
        kp.append(k_n); vp.append(v_n); sp.append(s_n); cp.append(c_n)
        past_k = cache_k[l][page_table].reshape(dec_b, past_len, H_DIFF, 2, DH_DIFF)
        past_v = cache_v[l][page_table].reshape(dec_b, past_len, H_DIFF, 2 * DH_DIFF)
        hs, k_n, v_n, s_n, c_n = decoder_layer(hs, p_sample[l], state_conv[l], state_ssm[l], past_k, past_v,
                                               past_len, lam_init, **wl)
        ks_.append(k_n); vs_.append(v_n); ss_.append(s_n); cs_.append(c_n)
    y_prompt = rmsnorm(hp, g_final)
    y_sample = rmsnorm(hs, g_final)
    return (y_prompt, y_sample, jnp.stack(kp), jnp.stack(vp), jnp.stack(sp), jnp.stack(cp),
            jnp.stack(ks_), jnp.stack(vs_), jnp.stack(ss_), jnp.stack(cs_))
```

```python
import functools
import math

import jax
import jax.numpy as jnp
from jax import lax
from jax.experimental import pallas as pl
from jax.experimental.pallas import tpu as pltpu

F32 = jnp.float32
BF16 = jnp.bfloat16
I32 = jnp.int32

D_MODEL = 1024
D_SSM = 512
D_DIFF = 512
P_SSM = 64
H_SSM = 8
N_STATE = 128
SSM_GROUPS = 2
HEADS_PER_GROUP = H_SSM // SSM_GROUPS
CONV_W = 4
CONV_DIM = D_SSM + 2 * SSM_GROUPS * N_STATE
DH = 64
H_DIFF = 4
N_BUCKETS = 32
REL_MAX_DIST = 128
N_KEYS = 128
PEER_HEADS = 8
PEER_TOPK = 16
D_KEY = 256
PLE_DIM = 256
NEG_INF = -1e30
RMS_EPS = 1e-6
LAM_INIT = 0.8 - 0.6 * math.exp(-0.3 * 0)

LANES = 128
SUBLANES = 8
VMEM_LIMIT = 56 * 1024 * 1024

SSD_CHUNK = 128
ROW_TILE = 256
ATTN_TILE = 512
PEER_TILE = 128
SLOTS = PEER_HEADS * PEER_TOPK


def _params(*sem):
    return pltpu.CompilerParams(dimension_semantics=sem, vmem_limit_bytes=VMEM_LIMIT)


def _const_spec(shape):
    nd = len(shape)
    return pl.BlockSpec(shape, lambda *_: (0,) * nd)


def _rms(x, g):
    return x * lax.rsqrt(jnp.mean(x * x, axis=-1, keepdims=True) + RMS_EPS) * g


def _dot(a, b):
    return jnp.dot(a, b, preferred_element_type=F32)


def _dot_nt(a, b):
    return lax.dot_general(a, b, (((1,), (1,)), ((), ())), preferred_element_type=F32)


def _dot_tn(a, b):
    return lax.dot_general(a, b, (((0,), (0,)), ((), ())), preferred_element_type=F32)


def _in_proj_kernel(x_ref, g_ref, wz_ref, wx_ref, wq_ref, wk_ref, wv_ref, wdt_ref,
                    z_ref, xbc_ref, q2_ref, k_ref, v_ref, kb_ref, vb_ref, dt_ref):
    a = _rms(x_ref[...], g_ref[...]).astype(BF16)
    z_ref[...] = _dot(a, wz_ref[...])
    xbc_ref[...] = _dot(a, wx_ref[...])
    dt_ref[...] = _dot(a, wdt_ref[...])
    k = _dot(a, wk_ref[...])
    v = _dot(a, wv_ref[...])
    k_ref[...] = k
    v_ref[...] = v
    kb_ref[...] = k.astype(BF16)
    vb_ref[...] = v.astype(BF16)
    q = _dot(a, wq_ref[...]).astype(BF16) * (DH ** -0.5)
    lane = lax.broadcasted_iota(I32, q.shape, 1)
    first_map = (lane % (2 * DH)) < DH
    zero = jnp.zeros_like(q)
    q2_ref[0] = jnp.where(first_map, q, zero)
    q2_ref[1] = jnp.where(first_map, zero, q)


def in_proj(x, g, wz, wx, wq, wk, wv, wdt):
    t = x.shape[0]
    tm = min(ROW_TILE, t)
    row = lambda n: pl.BlockSpec((tm, n), lambda i: (i, 0))
    out_shape = (
        jax.ShapeDtypeStruct((t, D_SSM), F32),
        jax.ShapeDtypeStruct((t, CONV_DIM), F32),
        jax.ShapeDtypeStruct((2, t, D_DIFF), BF16),
        jax.ShapeDtypeStruct((t, D_DIFF), F32),
        jax.ShapeDtypeStruct((t, D_DIFF), F32),
        jax.ShapeDtypeStruct((t, D_DIFF), BF16),
        jax.ShapeDtypeStruct((t, D_DIFF), BF16),
        jax.ShapeDtypeStruct((t, LANES), F32),
    )
    return pl.pallas_call(
        _in_proj_kernel,
        grid=(t // tm,),
        in_specs=[row(D_MODEL), _const_spec((1, D_MODEL)), _const_spec(wz.shape), _const_spec(wx.shape),
                  _const_spec(wq.shape), _const_spec(wk.shape), _const_spec(wv.shape), _const_spec(wdt.shape)],
        out_specs=(row(D_SSM), row(CONV_DIM), pl.BlockSpec((2, tm, D_DIFF), lambda i: (0, i, 0)),
                   row(D_DIFF), row(D_DIFF), row(D_DIFF), row(D_DIFF), row(LANES)),
        out_shape=out_shape,
        compiler_params=_params("parallel"),
        name="in_proj",
    )(x, g, wz, wx, wq, wk, wv, wdt)


def _ssd_kernel(xbc_ref, z_ref, dt_ref, cbuf_ref, s0_ref, cw_ref, cb_ref, dtb_ref, alog_ref, dskip_ref, gn_ref,
                y_ref, sfin_ref, win_ref, state_ref, *, seq_len):
    c = pl.program_id(1)
    lc = SSD_CHUNK

    @pl.when(c == 0)
    def _():
        win_ref[0:SUBLANES, :] = cbuf_ref[0]
        state_ref[...] = s0_ref[0]

    win_ref[SUBLANES:SUBLANES + lc, :] = xbc_ref[0]
    base = SUBLANES - (CONV_W - 1)
    conv = cb_ref[...]
    for j in range(CONV_W):
        conv = conv + cw_ref[j:j + 1, :] * win_ref[base + j:base + j + lc, :]
    win_ref[0:SUBLANES, :] = win_ref[lc:lc + SUBLANES, :]
    xc = conv * jax.nn.sigmoid(conv)

    pos = c * lc + lax.broadcasted_iota(I32, (lc, LANES), 0)
    dt = jax.nn.softplus(dt_ref[0] + dtb_ref[...])
    dt = jnp.where(pos < seq_len, dt, 0.0)
    adt = -jnp.exp(alog_ref[...]) * dt
    row = lax.broadcasted_iota(I32, (lc, lc), 0)
    col = lax.broadcasted_iota(I32, (lc, lc), 1)
    causal = row >= col
    tri = causal.astype(F32)
    cs = jnp.dot(tri, adt, preferred_element_type=F32, precision=lax.Precision.HIGHEST)
    cs_t = cs.T
    cs_last = cs[lc - 1:lc, :]

    ys = []
    for g in range(SSM_GROUPS):
        b0 = D_SSM + g * N_STATE
        c0 = D_SSM + SSM_GROUPS * N_STATE + g * N_STATE
        bg = xc[:, b0:b0 + N_STATE]
        cg = xc[:, c0:c0 + N_STATE].astype(BF16)
        cb = _dot_nt(cg, bg.astype(BF16))
        for hh in range(HEADS_PER_GROUP):
            h = g * HEADS_PER_GROUP + hh
            xs = xc[:, h * P_SSM:(h + 1) * P_SSM]
            cs_h = cs[:, h:h + 1]
            decay = jnp.where(causal, jnp.exp(jnp.where(causal, cs_h - cs_t[h:h + 1, :], 0.0)), 0.0)
            xdt = xs * dt[:, h:h + 1]
            y = _dot((cb * decay).astype(BF16), xdt.astype(BF16))
            st = state_ref[h]
            y = y + jnp.exp(cs_h) * _dot_nt(cg, st.astype(BF16))
            y = y + dskip_ref[:, h:h + 1] * xs
            ys.append(y)
            to_end = jnp.exp(cs_last[:, h:h + 1] - cs_h)
            new = _dot_tn((xdt * to_end).astype(BF16), bg.astype(BF16))
            state_ref[h] = jnp.exp(cs_last[:, h:h + 1]) * st + new
    y = jnp.concatenate(ys, axis=1)
    zz = z_ref[0]
    y = y * (zz * jax.nn.sigmoid(zz))
    gw = D_SSM // SSM_GROUPS
    outs = []
    for g in range(SSM_GROUPS):
        yg = y[:, g * gw:(g + 1) * gw]
        outs.append(yg * lax.rsqrt(jnp.mean(yg * yg, axis=-1, keepdims=True) + RMS_EPS))
    y_ref[0] = (jnp.concatenate(outs, axis=1) * gn_ref[...]).astype(BF16)

    @pl.when(c == pl.num_programs(1) - 1)
    def _():
        sfin_ref[0] = state_ref[...]


def ssd(xbc, z, dt, cbuf, s0, cw, cb, dtb, alog, dskip, gn, seq_len):
    b, lp, _ = xbc.shape
    lc = SSD_CHUNK
    seq = lambda n: pl.BlockSpec((1, lc, n), lambda i, c: (i, c, 0))
    return pl.pallas_call(
        functools.partial(_ssd_kernel, seq_len=seq_len),
        grid=(b, lp // lc),
        in_specs=[seq(CONV_DIM), seq(D_SSM), seq(LANES),
                  pl.BlockSpec((1, SUBLANES, CONV_DIM), lambda i, c: (i, 0, 0)),
                  pl.BlockSpec((1, H_SSM, P_SSM, N_STATE), lambda i, c: (i, 0, 0, 0)),
                  _const_spec(cw.shape), _const_spec(cb.shape), _const_spec(dtb.shape), _const_spec(alog.shape),
                  _const_spec(dskip.shape), _const_spec(gn.shape)],
        out_specs=(seq(D_SSM), pl.BlockSpec((1, H_SSM, P_SSM, N_STATE), lambda i, c: (i, 0, 0, 0))),
        out_shape=(jax.ShapeDtypeStruct((b, lp, D_SSM), BF16),
                   jax.ShapeDtypeStruct((b, H_SSM, P_SSM, N_STATE), F32)),
        scratch_shapes=[pltpu.VMEM((SUBLANES + lc, CONV_DIM), F32), pltpu.VMEM((H_SSM, P_SSM, N_STATE), F32)],
        compiler_params=_params("parallel", "arbitrary"),
        name="ssd",
    )(xbc, z, dt, cbuf, s0, cw, cb, dtb, alog, dskip, gn)


def _softmax_tile(q, k, v, bias, m_ref, l_ref, acc_ref):
    s = _dot_nt(q, k)
    if bias is not None:
        s = s + bias
    m_old = m_ref[...]
    m_new = jnp.maximum(m_old, jnp.max(s, axis=-1, keepdims=True))
    alpha = jnp.exp(m_old - m_new)
    p = jnp.exp(s - m_new)
    l_ref[...] = alpha * l_ref[...] + jnp.sum(p, axis=-1, keepdims=True)
    acc_ref[...] = alpha * acc_ref[...] + _dot(p.astype(BF16), v)
    m_ref[...] = m_new


def _softmax_init(m_ref, l_ref, acc_ref):
    m_ref[...] = jnp.full(m_ref.shape, NEG_INF, F32)
    l_ref[...] = jnp.zeros(l_ref.shape, F32)
    acc_ref[...] = jnp.zeros(acc_ref.shape, F32)


def _attn_kernel(lam_ref, q_ref, k_ref, v_ref, band_ref, gd_ref, o_ref, m_ref, l_ref, acc_ref, *, tq):
    i = pl.program_id(2)
    q = q_ref[...].reshape(2 * tq, 2 * DH)
    _softmax_init(m_ref, l_ref, acc_ref)

    def kv(j):
        start = pl.multiple_of(j * tq, tq)
        return k_ref[0, pl.ds(start, tq), :], v_ref[0, pl.ds(start, tq), :]

    def band(o):
        bias = band_ref[0, o]
        return jnp.concatenate([bias, bias], axis=0)

    def far(j, carry):
        k, v = kv(j)
        _softmax_tile(q, k, v, None, m_ref, l_ref, acc_ref)
        return carry

    lax.fori_loop(0, jnp.maximum(i - 1, 0), far, 0)

    @pl.when(i >= 1)
    def _():
        k, v = kv(i - 1)
        _softmax_tile(q, k, v, band(1), m_ref, l_ref, acc_ref)

    k, v = kv(i)
    _softmax_tile(q, k, v, band(0), m_ref, l_ref, acc_ref)

    o = acc_ref[...] / l_ref[...]
    o = o[:tq] - lam_ref[0] * o[tq:]
    o_ref[0] = (_rms(o, gd_ref[...]) * (1.0 - LAM_INIT)).astype(BF16)


def prompt_attention(lam, q2, kb, vb, band, gd, batch, seq, tq):
    nq = seq // tq
    head = 2 * DH
    return pl.pallas_call(
        functools.partial(_attn_kernel, tq=tq),
        grid=(batch, H_DIFF, nq),
        in_specs=[pl.BlockSpec(memory_space=pltpu.SMEM),
                  pl.BlockSpec((2, tq, head), lambda b, h, i: (0, b * nq + i, h)),
                  pl.BlockSpec((1, seq, head), lambda b, h, i: (b, 0, h)),
                  pl.BlockSpec((1, seq, head), lambda b, h, i: (b, 0, h)),
                  pl.BlockSpec((1, 2, tq, tq), lambda b, h, i: (h, 0, 0, 0)),
                  _const_spec((1, head))],
        out_specs=pl.BlockSpec((1, tq, head), lambda b, h, i: (b, i, h)),
        out_shape=jax.ShapeDtypeStruct((batch, seq, D_DIFF), BF16),
        scratch_shapes=[pltpu.VMEM((2 * tq, 1), F32), pltpu.VMEM((2 * tq, 1), F32), pltpu.VMEM((2 * tq, head), F32)],
        compiler_params=_params("parallel", "parallel", "arbitrary"),
        name="prompt_attention",
    )(lam, q2, kb, vb, band, gd)


DEC_ROWS = 2 * H_DIFF


def _dec_attn_kernel(pt_ref, lam_ref, q_ref, kc_ref, vc_ref, kn_ref, vn_ref, band_ref, gd_ref, o_ref,
                     m_ref, l_ref, acc_ref, *, n_pages, n_q):
    j = pl.program_id(1)
    q = q_ref[0]

    @pl.when(j == 0)
    def _():
        _softmax_init(m_ref, l_ref, acc_ref)

    @pl.when(j < n_pages - 1)
    def _():
        _softmax_tile(q, kc_ref[0].astype(BF16), vc_ref[0].astype(BF16), None, m_ref, l_ref, acc_ref)

    @pl.when(j == n_pages - 1)
    def _():
        _softmax_tile(q, kc_ref[0].astype(BF16), vc_ref[0].astype(BF16), band_ref[0], m_ref, l_ref, acc_ref)

    @pl.when(j == n_pages)
    def _():
        _softmax_tile(q, kn_ref[0], vn_ref[0], band_ref[1], m_ref, l_ref, acc_ref)
        rows = n_q * DEC_ROWS
        r = lax.broadcasted_iota(I32, (rows, D_DIFF), 0)
        lane = lax.broadcasted_iota(I32, (rows, D_DIFF), 1)
        own = (lane // (2 * DH)) == ((r % DEC_ROWS) // 2)
        w = jnp.where(r % 2 == 0, 1.0, -lam_ref[0])
        o = jnp.where(own, acc_ref[...] / l_ref[...] * w, 0.0)
        o = jnp.sum(o.reshape(n_q, DEC_ROWS, D_DIFF), axis=1)
        outs = []
        for h in range(H_DIFF):
            oh = o[:, h * 2 * DH:(h + 1) * 2 * DH]
            outs.append(_rms(oh, gd_ref[...]) * (1.0 - LAM_INIT))
        o_ref[0] = jnp.concatenate(outs, axis=1).astype(BF16)


def sample_attention(page_table, lam, qst, cache_k, cache_v, kn, vn, band, gd, n_q):
    b, n_pages = page_table.shape
    page = cache_k.shape[1]
    rows = n_q * DEC_ROWS
    cache_spec = pl.BlockSpec((1, page, D_DIFF), lambda i, j, pt: (pt[i, jnp.minimum(j, n_pages - 1)], 0, 0))
    new_spec = pl.BlockSpec((1, page, D_DIFF), lambda i, j, pt: (i, 0, 0))
    grid_spec = pltpu.PrefetchScalarGridSpec(
        num_scalar_prefetch=1,
        grid=(b, n_pages + 1),
        in_specs=[pl.BlockSpec(memory_space=pltpu.SMEM),
                  pl.BlockSpec((1, rows, D_DIFF), lambda i, j, pt: (i, 0, 0)),
                  cache_spec, cache_spec, new_spec, new_spec,
                  pl.BlockSpec((2, rows, page), lambda i, j, pt: (0, 0, 0)),
                  pl.BlockSpec((1, 2 * DH), lambda i, j, pt: (0, 0))],
        out_specs=pl.BlockSpec((1, n_q, D_DIFF), lambda i, j, pt: (i, 0, 0)),
        scratch_shapes=[pltpu.VMEM((rows, 1), F32), pltpu.VMEM((rows, 1), F32), pltpu.VMEM((rows, D_DIFF), F32)],
    )
    return pl.pallas_call(
        functools.partial(_dec_attn_kernel, n_pages=n_pages, n_q=n_q),
        grid_spec=grid_spec,
        out_shape=jax.ShapeDtypeStruct((b, n_q, D_DIFF), BF16),
        compiler_params=_params("parallel", "arbitrary"),
        name="sample_attention",
    )(page_table, lam, qst, cache_k, cache_v, kn, vn, band, gd)


def _mid_kernel(x_ref, ys_ref, yd_ref, wa_ref, wb_ref, g_ref, wq_ref, h_ref, m_ref, qp_ref):
    h = x_ref[...] + _dot(ys_ref[...], wa_ref[...]) + _dot(yd_ref[...], wb_ref[...])
    h_ref[...] = h
    m = _rms(h, g_ref[...]).astype(BF16)
    m_ref[...] = m.astype(F32)
    qp_ref[...] = _dot(m, wq_ref[...]).astype(BF16)


def mid(x, ys, yd, wa, wb, g, wq):
    t = x.shape[0]
    tm = min(ROW_TILE, t)
    row = lambda n: pl.BlockSpec((tm, n), lambda i: (i, 0))
    nq = wq.shape[1]
    return pl.pallas_call(
        _mid_kernel,
        grid=(t // tm,),
        in_specs=[row(D_MODEL), row(D_SSM), row(D_DIFF), _const_spec(wa.shape), _const_spec(wb.shape),
                  _const_spec((1, D_MODEL)), _const_spec(wq.shape)],
        out_specs=(row(D_MODEL), row(D_MODEL), row(nq)),
        out_shape=(jax.ShapeDtypeStruct((t, D_MODEL), F32), jax.ShapeDtypeStruct((t, D_MODEL), F32),
                   jax.ShapeDtypeStruct((t, nq), BF16)),
        compiler_params=_params("parallel"),
        name="mid",
    )(x, ys, yd, wa, wb, g, wq)


def _top16(vals):
    rows = lax.broadcasted_iota(I32, vals.shape, 0)
    big = jnp.int32(vals.shape[0])
    out_v, out_i = [], []
    cur = vals
    for _ in range(PEER_TOPK):
        mx = jnp.max(cur, axis=0, keepdims=True)
        at = jnp.min(jnp.where(cur == mx, rows, big), axis=0, keepdims=True)
        out_v.append(mx)
        out_i.append(at)
        cur = jnp.where(rows == at, -jnp.inf, cur)
    return jnp.concatenate(out_v, axis=0), jnp.concatenate(out_i, axis=0)


def _topk_kernel(qp_ref, sk_ref, tile_ref, shift_ref, gate_ref):
    tb = qp_ref.shape[0]
    half = D_KEY // 2
    idx_rows, gate_rows = [], []
    for h in range(PEER_HEADS):
        s, ix = [], []
        for c in range(2):
            qh = qp_ref[:, (2 * h + c) * half:(2 * h + c + 1) * half]
            sc = _dot_nt(sk_ref[c], qh)
            v, i = _top16(sc)
            s.append(v)
            ix.append(i)
        cand = jnp.concatenate([s[0][a:a + 1, :] + s[1] for a in range(PEER_TOPK)], axis=0)
        cidx = jnp.concatenate([ix[0][a:a + 1, :] * N_KEYS + ix[1] for a in range(PEER_TOPK)], axis=0)
        sc, pos = _top16(cand)
        prow = lax.broadcasted_iota(I32, cand.shape, 0)
        idx = jnp.concatenate(
            [jnp.max(jnp.where(prow == pos[k:k + 1, :], cidx, -1), axis=0, keepdims=True) for k in range(PEER_TOPK)],
            axis=0)
        e = jnp.exp(sc - sc[0:1, :])
        gate_rows.append(e / jnp.sum(e, axis=0, keepdims=True))
        idx_rows.append(idx)
    idx = jnp.concatenate(idx_rows, axis=0).T
    gate_ref[...] = jnp.concatenate(gate_rows, axis=0).T
    tile_ref[...] = idx >> 1
    shift_ref[...] = ((idx & 1) ^ 1) << 4


def peer_topk(qp, sk):
    t = qp.shape[0]
    tb = min(PEER_TILE, t)
    out = pl.BlockSpec((tb, SLOTS), lambda i: (i, 0))
    return pl.pallas_call(
        _topk_kernel,
        grid=(t // tb,),
        in_specs=[pl.BlockSpec((tb, qp.shape[1]), lambda i: (i, 0)), _const_spec(sk.shape)],
        out_specs=(out, out, out),
        out_shape=(jax.ShapeDtypeStruct((t, SLOTS), I32), jax.ShapeDtypeStruct((t, SLOTS), I32),
                   jax.ShapeDtypeStruct((t, SLOTS), F32)),
        compiler_params=_params("parallel"),
        name="peer_topk",
    )(qp, sk)


HI_MASK = -65536


def _pack_kernel(w_ref, o_ref):
    n = o_ref.shape[0]
    even = w_ref[pl.ds(0, n, stride=2), :].astype(BF16).astype(F32)
    odd = w_ref[pl.ds(1, n, stride=2), :].astype(BF16).astype(F32)
    lo = (lax.bitcast_convert_type(even, I32) >> 16) & 0xFFFF
    o_ref[...] = lo | (lax.bitcast_convert_type(odd, I32) & HI_MASK)


def pack_table(w):
    n, d = w.shape
    rows = 2048
    packed = pl.pallas_call(
        _pack_kernel,
        grid=(n // (2 * rows), d // LANES),
        in_specs=[pl.BlockSpec((2 * rows, LANES), lambda i, c: (i, c))],
        out_specs=pl.BlockSpec((rows, LANES), lambda i, c: (i, c)),
        out_shape=jax.ShapeDtypeStruct((n // 2, d), I32),
        compiler_params=_params("parallel", "parallel"),
        name="pack_table",
    )(w)
    return packed.reshape(n // 2, d // LANES, LANES)


def _expert_row(tab_ref, tile_ref, shift_ref, t, j):
    word = tab_ref[tile_ref[t, j]]
    return lax.bitcast_convert_type((word << shift_ref[t, j]) & HI_MASK, F32)


UNIT = SUBLANES


def _butterfly(parts, add, roll, select):
    step = SUBLANES // 2
    while len(parts) > 1:
        parts = [select(step, add(a, roll(a, step)), add(b, roll(b, SUBLANES - step)))
                 for a, b in zip(parts[0::2], parts[1::2])]
        step //= 2
    return parts[0]


def _butterfly_order():
    tiles = [[{p}] * SUBLANES for p in range(SUBLANES)]
    out = _butterfly(
        tiles,
        add=lambda a, b: [x | y for x, y in zip(a, b)],
        roll=lambda a, k: [a[(i - k) % SUBLANES] for i in range(SUBLANES)],
        select=lambda step, a, b: [a[i] if (i // step) % 2 else b[i] for i in range(SUBLANES)])
    order = [0] * SUBLANES
    for sublane, owners in enumerate(out):
        (p,) = owners
        order[p] = sublane
    return order


BUTTERFLY_ORDER = _butterfly_order()


def _rows_to_sublanes(parts):
    sub = lax.broadcasted_iota(I32, (SUBLANES, LANES), 0)
    placed = [None] * SUBLANES
    for i, part in enumerate(parts):
        placed[BUTTERFLY_ORDER.index(i)] = part
    return _butterfly(placed, add=lambda a, b: a + b, roll=lambda a, k: pltpu.roll(a, k, 0),
                      select=lambda step, a, b: jnp.where((sub // step) % 2 == 1, a, b))


def _peer_u_kernel(tile_ref, shift_ref, m_ref, gate_ref, tab_ref, w_ref, ht_ref):
    tb = m_ref.shape[0]
    lane = lax.broadcasted_iota(I32, (SUBLANES, LANES), 1)

    def token(t, carry):
        x = m_ref[t]
        cols = []
        for u in range(SLOTS // UNIT):
            parts = [_expert_row(tab_ref, tile_ref, shift_ref, t, u * UNIT + e) * x for e in range(UNIT)]
            cols.append(jnp.sum(_rows_to_sublanes(parts), axis=1, keepdims=True))
        col = jnp.concatenate(cols, axis=0)
        lanes = lax.broadcasted_iota(I32, (SLOTS, tb), 1)
        ht_ref[...] = jnp.where(lanes == t, col, ht_ref[...])
        return carry

    lax.fori_loop(0, tb, token, 0)
    h = ht_ref[...].T
    act = 0.5 * h * (1.0 + lax.erf(h * (2.0 ** -0.5)))
    w_ref[...] = gate_ref[...] * act


def peer_u(tile, shift, m3, gate, tab):
    t = m3.shape[0]
    tb = min(PEER_TILE, t)
    smem = pl.BlockSpec((tb, SLOTS), lambda i: (i, 0), memory_space=pltpu.SMEM)
    return pl.pallas_call(
        _peer_u_kernel,
        grid=(t // tb,),
        in_specs=[smem, smem,
                  pl.BlockSpec((tb, SUBLANES, LANES), lambda i: (i, 0, 0)),
                  pl.BlockSpec((tb, SLOTS), lambda i: (i, 0)),
                  pl.BlockSpec(tab.shape, lambda i: (0, 0, 0), pipeline_mode=pl.Buffered(1))],
        out_specs=pl.BlockSpec((tb, SLOTS), lambda i: (i, 0)),
        out_shape=jax.ShapeDtypeStruct((t, SLOTS), F32),
        scratch_shapes=[pltpu.VMEM((SLOTS, tb), F32)],
        compiler_params=_params("arbitrary"),
        name="peer_u",
    )(tile, shift, m3, gate, tab)


N_ACC = 4


def _peer_v_kernel(tile_ref, shift_ref, w_ref, tab_ref, o_ref):
    tb = o_ref.shape[0]

    def token(t, carry):
        acc = [None] * N_ACC
        for j in range(SLOTS):
            term = w_ref[t, j] * _expert_row(tab_ref, tile_ref, shift_ref, t, j)
            acc[j % N_ACC] = term if acc[j % N_ACC] is None else acc[j % N_ACC] + term
        o_ref[t] = (acc[0] + acc[1]) + (acc[2] + acc[3])
        return carry

    lax.fori_loop(0, tb, token, 0)


def peer_v(tile, shift, w, tab):
    t = w.shape[0]
    tb = min(PEER_TILE, t)
    smem = pl.BlockSpec((tb, SLOTS), lambda i: (i, 0), memory_space=pltpu.SMEM)
    return pl.pallas_call(
        _peer_v_kernel,
        grid=(t // tb,),
        in_specs=[smem, smem, smem,
                  pl.BlockSpec(tab.shape, lambda i: (0, 0, 0), pipeline_mode=pl.Buffered(1))],
        out_specs=pl.BlockSpec((tb, SUBLANES, LANES), lambda i: (i, 0, 0)),
        out_shape=jax.ShapeDtypeStruct((t, SUBLANES, LANES), F32),
        compiler_params=_params("arbitrary"),
        name="peer_v",
    )(tile, shift, w, tab)


def _ple_kernel(h_ref, peer_ref, p_ref, wp_ref, wg_ref, gp_ref, gf_ref, y_ref):
    h = h_ref[...] + peer_ref[...]
    gate = jax.nn.sigmoid(_dot(_rms(h, gp_ref[...]).astype(BF16), wg_ref[...]))
    h = h + _dot(p_ref[...].astype(BF16), wp_ref[...]) * gate
    y_ref[...] = _rms(h, gf_ref[...])


def ple(h, peer_out, p, wp, wg, gp, gf):
    t = h.shape[0]
    tm = min(ROW_TILE, t)
    row = lambda n: pl.BlockSpec((tm, n), lambda i: (i, 0))
    return pl.pallas_call(
        _ple_kernel,
        grid=(t // tm,),
        in_specs=[row(D_MODEL), row(D_MODEL), row(PLE_DIM), _const_spec(wp.shape), _const_spec(wg.shape),
                  _const_spec((1, D_MODEL)), _const_spec((1, D_MODEL))],
        out_specs=row(D_MODEL),
        out_shape=jax.ShapeDtypeStruct((t, D_MODEL), F32),
        compiler_params=_params("parallel"),
        name="ple",
    )(h, peer_out, p, wp, wg, gp, gf)


def _t5_bucket(rel):
    n = jnp.maximum(rel, 0)
    max_exact = N_BUCKETS // 2
    nf = jnp.maximum(n, 1).astype(F32)
    large = max_exact + (jnp.log(nf / max_exact) / math.log(REL_MAX_DIST / max_exact)
                         * (N_BUCKETS - max_exact)).astype(I32)
    large = jnp.minimum(large, N_BUCKETS - 1)
    return jnp.where(n < max_exact, n, large)


def _band_bias(rel_bias, rel):
    far = rel_bias[N_BUCKETS - 1]
    b = rel_bias[_t5_bucket(rel)] - far
    b = jnp.where((rel >= 0)[..., None], b, NEG_INF)
    return jnp.moveaxis(b, -1, 0).astype(F32)


def _pad_rows(a, rows):
    return jnp.pad(a, ((0, 0), (0, rows - a.shape[1]), (0, 0)))


def _layer(x, p_l, conv_buf, ssm0, lam, w, attention):
    b, l, _ = x.shape
    t = b * l
    x2 = x.reshape(t, D_MODEL)
    z, xbc, q2, k, v, kb, vb, dt = in_proj(x2, w["g_mix"], w["wz"], w["wx"], w["wq"], w["wk"], w["wv"], w["wdt"])

    lp = -(-l // SSD_CHUNK) * SSD_CHUNK
    xbc3 = xbc.reshape(b, l, CONV_DIM)
    cbuf = jnp.pad(conv_buf, ((0, 0), (SUBLANES - (CONV_W - 1), 0), (0, 0)))
    y_ssm, ssm_final = ssd(_pad_rows(xbc3, lp), _pad_rows(z.reshape(b, l, D_SSM), lp),
                           _pad_rows(dt.reshape(b, l, LANES), lp), cbuf, ssm0,
                           w["conv_w"], w["conv_b"], w["dt_bias"], w["a_log"], w["d_skip"], w["g_ssm_norm"], l)
    y_ssm = y_ssm[:, :l].reshape(t, D_SSM)
    conv_new = jnp.concatenate([conv_buf, xbc3], axis=1)[:, -(CONV_W - 1):]

    y_diff = attention(q2, kb, vb)

    h1, m, qp = mid(x2, y_ssm, y_diff, w["wo_a"], w["wo_b"], w["g_ffn"], w["w_query"])
    tile, shift, gate = peer_topk(qp, w["sub_keys"])
    wts = peer_u(tile, shift, m.reshape(t, SUBLANES, LANES), gate, w["tab_u"])
    peer_out = peer_v(tile, shift, wts, w["tab_v"]).reshape(t, D_MODEL)
    y = ple(h1, peer_out, p_l.reshape(t, PLE_DIM), w["w_ple"], w["w_ple_gate"], w["g_ple"], w["g_final"])
    return (y.reshape(b, l, D_MODEL), k.reshape(b, l, H_DIFF, 2, DH), v.reshape(b, l, H_DIFF, 2 * DH),
            ssm_final, conv_new)


def kernel(x_prompt, x_sample, cache_k, cache_v, state_ssm, state_conv, page_table, p_prompt, p_sample, g_mix, w_in, conv_w, conv_b, dt_bias, a_log, d_skip, g_ssm_norm, lam_q1, lam_k1, lam_q2, lam_k2, g_diff_norm, rel_bias, w_out, g_ffn, w_query, sub_keys, expert_u, expert_v, w_ple, w_ple_gate, g_ple, g_final):
    assert w_in.shape[0] == 1, "single-layer problem"
    bsz, seq, _ = x_prompt.shape
    dec_b, dec_l, _ = x_sample.shape
    n_pages = page_table.shape[1]
    page = cache_k.shape[2]
    past = n_pages * page

    wi = w_in[0].astype(BF16)
    o_x = D_SSM
    o_dt = o_x + CONV_DIM
    o_q = o_dt + H_SSM
    o_k = o_q + D_DIFF
    o_v = o_k + D_DIFF
    lanes_of = lambda a: jnp.pad(a.astype(F32)[None, :], ((0, 0), (0, LANES - a.shape[0])))
    w = dict(
        g_mix=g_mix[0][None], wz=wi[:, :o_x], wx=wi[:, o_x:o_dt],
        wdt=jnp.pad(wi[:, o_dt:o_q], ((0, 0), (0, LANES - H_SSM))),
        wq=wi[:, o_q:o_k], wk=wi[:, o_k:o_v], wv=wi[:, o_v:],
        conv_w=conv_w[0], conv_b=conv_b[0][None], dt_bias=lanes_of(dt_bias[0]), a_log=lanes_of(a_log[0]),
        d_skip=lanes_of(d_skip[0]), g_ssm_norm=g_ssm_norm[0][None],
        wo_a=w_out[0][:D_SSM].astype(BF16), wo_b=w_out[0][D_SSM:].astype(BF16), g_ffn=g_ffn[0][None],
        w_query=w_query[0].astype(BF16), sub_keys=sub_keys[0].astype(BF16),
        tab_u=pack_table(expert_u[0]), tab_v=pack_table(expert_v[0]),
        w_ple=w_ple[0].astype(BF16), w_ple_gate=w_ple_gate[0].astype(BF16), g_ple=g_ple[0][None],
        g_final=g_final[None],
    )
    gd = g_diff_norm[0][None]
    lam = (jnp.exp(jnp.sum(lam_q1[0] * lam_k1[0])) - jnp.exp(jnp.sum(lam_q2[0] * lam_k2[0])) + LAM_INIT).reshape(1)

    tq = min(ATTN_TILE, seq)
    ii = jnp.arange(tq, dtype=I32)
    rel = jnp.stack([ii[:, None] - ii[None, :], tq + ii[:, None] - ii[None, :]])
    band_p = _band_bias(rel_bias, rel)

    def attn_prompt(q2, kb, vb):
        o = prompt_attention(lam, q2, kb.reshape(bsz, seq, D_DIFF), vb.reshape(bsz, seq, D_DIFF), band_p, gd,
                             bsz, seq, tq)
        return o.reshape(bsz * seq, D_DIFF)

    yp, kp, vp, sp, cp = _layer(x_prompt, p_prompt[0], jnp.zeros((bsz, CONV_W - 1, CONV_DIM), F32),
                                jnp.zeros((bsz, H_SSM, P_SSM, N_STATE), F32), lam, w, attn_prompt)

    rows = dec_l * DEC_ROWS
    qpos = past + jnp.arange(rows, dtype=I32) // DEC_ROWS
    head_of = (jnp.arange(rows, dtype=I32) % DEC_ROWS) // 2
    kk = jnp.arange(page, dtype=I32)
    rel_last = qpos[:, None] - (past - page + kk)[None, :]
    rel_new = jnp.where(kk[None, :] < dec_l, qpos[:, None] - (past + kk)[None, :], -1)
    band_all = _band_bias(rel_bias, jnp.stack([rel_last, rel_new]))
    band_s = jnp.take_along_axis(band_all, head_of[None, None, :, None], axis=0)[0]
    lane_slot = jnp.arange(D_DIFF, dtype=I32) // DH
    row_slot = jnp.arange(rows, dtype=I32) % DEC_ROWS
    q_mask = (lane_slot[None, :] == row_slot[:, None])

    def attn_sample(q2, kb, vb):
        q = (q2[0] + q2[1]).reshape(dec_b, dec_l, D_DIFF)
        qst = jnp.where(q_mask[None], jnp.repeat(q, DEC_ROWS, axis=1), jnp.zeros((), BF16))
        kn = _pad_rows(kb.reshape(dec_b, dec_l, D_DIFF), page)
        vn = _pad_rows(vb.reshape(dec_b, dec_l, D_DIFF), page)
        o = sample_attention(page_table, lam, qst, cache_k[0].reshape(-1, page, D_DIFF),
                             cache_v[0].reshape(-1, page, D_DIFF), kn, vn, band_s, gd, dec_l)
        return o.reshape(dec_b * dec_l, D_DIFF)

    ys, ks, vs, ss, cs = _layer(x_sample, p_sample[0], state_conv[0], state_ssm[0], lam, w, attn_sample)

    return (yp, ys, kp[None], vp[None], sp[None], cp[None], ks[None], vs[None], ss[None], cs[None])
```

```python
import functools
import math

import jax
import jax.numpy as jnp
from jax import lax
from jax.experimental import pallas as pl
from jax.experimental.pallas import tpu as pltpu

F32 = jnp.float32
BF16 = jnp.bfloat16
I32 = jnp.int32

D_MODEL = 1024
D_SSM = 512
D_DIFF = 512
P_SSM = 64
H_SSM = 8
N_STATE = 128
SSM_GROUPS = 2
HEADS_PER_GROUP = H_SSM // SSM_GROUPS
CONV_W = 4
CONV_DIM = D_SSM + 2 * SSM_GROUPS * N_STATE
DH = 64
H_DIFF = 4
N_BUCKETS = 32
REL_MAX_DIST = 128
N_KEYS = 128
PEER_HEADS = 8
PEER_TOPK = 16
D_KEY = 256
PLE_DIM = 256
NEG_INF = -1e30
RMS_EPS = 1e-6
LAM_INIT = 0.8 - 0.6 * math.exp(-0.3 * 0)

LANES = 128
SUBLANES = 8
VMEM_LIMIT = 56 * 1024 * 1024

SSD_CHUNK = 128
ROW_TILE = 256
ATTN_TILE = 512
PEER_TILE = 128
EXPERT_TILE = 16
SLOTS = PEER_HEADS * PEER_TOPK


def _params(*sem):
    return pltpu.CompilerParams(dimension_semantics=sem, vmem_limit_bytes=VMEM_LIMIT)


def _const_spec(shape):
    nd = len(shape)
    return pl.BlockSpec(shape, lambda *_: (0,) * nd)


def _rms(x, g):
    return x * lax.rsqrt(jnp.mean(x * x, axis=-1, keepdims=True) + RMS_EPS) * g


def _dot(a, b):
    return jnp.dot(a, b, preferred_element_type=F32)


def _dot_nt(a, b):
    return lax.dot_general(a, b, (((1,), (1,)), ((), ())), preferred_element_type=F32)


def _dot_tn(a, b):
    return lax.dot_general(a, b, (((0,), (0,)), ((), ())), preferred_element_type=F32)


def _in_proj_kernel(x_ref, g_ref, wz_ref, wx_ref, wq_ref, wk_ref, wv_ref, wdt_ref,
                    z_ref, xbc_ref, q2_ref, k_ref, v_ref, kb_ref, vb_ref, dt_ref):
    a = _rms(x_ref[...], g_ref[...]).astype(BF16)
    z_ref[...] = _dot(a, wz_ref[...])
    xbc_ref[...] = _dot(a, wx_ref[...])
    dt_ref[...] = _dot(a, wdt_ref[...])
    k = _dot(a, wk_ref[...])
    v = _dot(a, wv_ref[...])
    k_ref[...] = k
    v_ref[...] = v
    kb_ref[...] = k.astype(BF16)
    vb_ref[...] = v.astype(BF16)
    q = _dot(a, wq_ref[...]).astype(BF16) * (DH ** -0.5)
    lane = lax.broadcasted_iota(I32, q.shape, 1)
    first_map = (lane % (2 * DH)) < DH
    zero = jnp.zeros_like(q)
    q2_ref[0] = jnp.where(first_map, q, zero)
    q2_ref[1] = jnp.where(first_map, zero, q)


def in_proj(x, g, wz, wx, wq, wk, wv, wdt):
    t = x.shape[0]
    tm = min(ROW_TILE, t)
    row = lambda n: pl.BlockSpec((tm, n), lambda i: (i, 0))
    out_shape = (
        jax.ShapeDtypeStruct((t, D_SSM), F32),
        jax.ShapeDtypeStruct((t, CONV_DIM), F32),
        jax.ShapeDtypeStruct((2, t, D_DIFF), BF16),
        jax.ShapeDtypeStruct((t, D_DIFF), F32),
        jax.ShapeDtypeStruct((t, D_DIFF), F32),
        jax.ShapeDtypeStruct((t, D_DIFF), BF16),
        jax.ShapeDtypeStruct((t, D_DIFF), BF16),
        jax.ShapeDtypeStruct((t, LANES), F32),
    )
    return pl.pallas_call(
        _in_proj_kernel,
        grid=(t // tm,),
        in_specs=[row(D_MODEL), _const_spec((1, D_MODEL)), _const_spec(wz.shape), _const_spec(wx.shape),
                  _const_spec(wq.shape), _const_spec(wk.shape), _const_spec(wv.shape), _const_spec(wdt.shape)],
        out_specs=(row(D_SSM), row(CONV_DIM), pl.BlockSpec((2, tm, D_DIFF), lambda i: (0, i, 0)),
                   row(D_DIFF), row(D_DIFF), row(D_DIFF), row(D_DIFF), row(LANES)),
        out_shape=out_shape,
        compiler_params=_params("parallel"),
        name="in_proj",
    )(x, g, wz, wx, wq, wk, wv, wdt)


def _ssd_kernel(xbc_ref, z_ref, dt_ref, cbuf_ref, s0_ref, cw_ref, cb_ref, dtb_ref, alog_ref, dskip_ref, gn_ref,
                y_ref, sfin_ref, win_ref, state_ref, *, seq_len):
    c = pl.program_id(1)
    lc = SSD_CHUNK

    @pl.when(c == 0)
    def _():
        win_ref[0:SUBLANES, :] = cbuf_ref[0]
        state_ref[...] = s0_ref[0]

    win_ref[SUBLANES:SUBLANES + lc, :] = xbc_ref[0]
    base = SUBLANES - (CONV_W - 1)
    conv = cb_ref[...]
    for j in range(CONV_W):
        conv = conv + cw_ref[j:j + 1, :] * win_ref[base + j:base + j + lc, :]
    win_ref[0:SUBLANES, :] = win_ref[lc:lc + SUBLANES, :]
    xc = conv * jax.nn.sigmoid(conv)

    pos = c * lc + lax.broadcasted_iota(I32, (lc, LANES), 0)
    dt = jax.nn.softplus(dt_ref[0] + dtb_ref[...])
    dt = jnp.where(pos < seq_len, dt, 0.0)
    adt = -jnp.exp(alog_ref[...]) * dt
    row = lax.broadcasted_iota(I32, (lc, lc), 0)
    col = lax.broadcasted_iota(I32, (lc, lc), 1)
    causal = row >= col
    tri = causal.astype(F32)
    cs = jnp.dot(tri, adt, preferred_element_type=F32, precision=lax.Precision.HIGHEST)
    cs_t = cs.T
    cs_last = cs[lc - 1:lc, :]

    ys = []
    for g in range(SSM_GROUPS):
        b0 = D_SSM + g * N_STATE
        c0 = D_SSM + SSM_GROUPS * N_STATE + g * N_STATE
        bg = xc[:, b0:b0 + N_STATE]
        cg = xc[:, c0:c0 + N_STATE].astype(BF16)
        cb = _dot_nt(cg, bg.astype(BF16))
        for hh in range(HEADS_PER_GROUP):
            h = g * HEADS_PER_GROUP + hh
            xs = xc[:, h * P_SSM:(h + 1) * P_SSM]
            cs_h = cs[:, h:h + 1]
            decay = jnp.where(causal, jnp.exp(jnp.where(causal, cs_h - cs_t[h:h + 1, :], 0.0)), 0.0)
            xdt = xs * dt[:, h:h + 1]
            y = _dot((cb * decay).astype(BF16), xdt.astype(BF16))
            st = state_ref[h]
            y = y + jnp.exp(cs_h) * _dot_nt(cg, st.astype(BF16))
            y = y + dskip_ref[:, h:h + 1] * xs
            ys.append(y)
            to_end = jnp.exp(cs_last[:, h:h + 1] - cs_h)
            new = _dot_tn((xdt * to_end).astype(BF16), bg.astype(BF16))
            state_ref[h] = jnp.exp(cs_last[:, h:h + 1]) * st + new
    y = jnp.concatenate(ys, axis=1)
    zz = z_ref[0]
    y = y * (zz * jax.nn.sigmoid(zz))
    gw = D_SSM // SSM_GROUPS
    outs = []
    for g in range(SSM_GROUPS):
        yg = y[:, g * gw:(g + 1) * gw]
        outs.append(yg * lax.rsqrt(jnp.mean(yg * yg, axis=-1, keepdims=True) + RMS_EPS))
    y_ref[0] = (jnp.concatenate(outs, axis=1) * gn_ref[...]).astype(BF16)

    @pl.when(c == pl.num_programs(1) - 1)
    def _():
        sfin_ref[0] = state_ref[...]


def ssd(xbc, z, dt, cbuf, s0, cw, cb, dtb, alog, dskip, gn, seq_len):
    b, lp, _ = xbc.shape
    lc = SSD_CHUNK
    seq = lambda n: pl.BlockSpec((1, lc, n), lambda i, c: (i, c, 0))
    return pl.pallas_call(
        functools.partial(_ssd_kernel, seq_len=seq_len),
        grid=(b, lp // lc),
        in_specs=[seq(CONV_DIM), seq(D_SSM), seq(LANES),
                  pl.BlockSpec((1, SUBLANES, CONV_DIM), lambda i, c: (i, 0, 0)),
                  pl.BlockSpec((1, H_SSM, P_SSM, N_STATE), lambda i, c: (i, 0, 0, 0)),
                  _const_spec(cw.shape), _const_spec(cb.shape), _const_spec(dtb.shape), _const_spec(alog.shape),
                  _const_spec(dskip.shape), _const_spec(gn.shape)],
        out_specs=(seq(D_SSM), pl.BlockSpec((1, H_SSM, P_SSM, N_STATE), lambda i, c: (i, 0, 0, 0))),
        out_shape=(jax.ShapeDtypeStruct((b, lp, D_SSM), BF16),
                   jax.ShapeDtypeStruct((b, H_SSM, P_SSM, N_STATE), F32)),
        scratch_shapes=[pltpu.VMEM((SUBLANES + lc, CONV_DIM), F32), pltpu.VMEM((H_SSM, P_SSM, N_STATE), F32)],
        compiler_params=_params("parallel", "arbitrary"),
        name="ssd",
    )(xbc, z, dt, cbuf, s0, cw, cb, dtb, alog, dskip, gn)


def _softmax_tile(q, k, v, bias, m_ref, l_ref, acc_ref):
    s = _dot_nt(q, k)
    if bias is not None:
        s = s + bias
    m_old = m_ref[...]
    m_new = jnp.maximum(m_old, jnp.max(s, axis=-1, keepdims=True))
    alpha = jnp.exp(m_old - m_new)
    p = jnp.exp(s - m_new)
    l_ref[...] = alpha * l_ref[...] + jnp.sum(p, axis=-1, keepdims=True)
    acc_ref[...] = alpha * acc_ref[...] + _dot(p.astype(BF16), v)
    m_ref[...] = m_new


def _softmax_init(m_ref, l_ref, acc_ref):
    m_ref[...] = jnp.full(m_ref.shape, NEG_INF, F32)
    l_ref[...] = jnp.zeros(l_ref.shape, F32)
    acc_ref[...] = jnp.zeros(acc_ref.shape, F32)


def _attn_kernel(lam_ref, q_ref, k_ref, v_ref, band_ref, gd_ref, o_ref, m_ref, l_ref, acc_ref, *, tq):
    i = pl.program_id(2)
    q = q_ref[...].reshape(2 * tq, 2 * DH)
    _softmax_init(m_ref, l_ref, acc_ref)

    def kv(j):
        start = pl.multiple_of(j * tq, tq)
        return k_ref[0, pl.ds(start, tq), :], v_ref[0, pl.ds(start, tq), :]

    def band(o):
        bias = band_ref[0, o]
        return jnp.concatenate([bias, bias], axis=0)

    def far(j, carry):
        k, v = kv(j)
        _softmax_tile(q, k, v, None, m_ref, l_ref, acc_ref)
        return carry

    lax.fori_loop(0, jnp.maximum(i - 1, 0), far, 0)

    @pl.when(i >= 1)
    def _():
        k, v = kv(i - 1)
        _softmax_tile(q, k, v, band(1), m_ref, l_ref, acc_ref)

    k, v = kv(i)
    _softmax_tile(q, k, v, band(0), m_ref, l_ref, acc_ref)

    o = acc_ref[...] / l_ref[...]
    o = o[:tq] - lam_ref[0] * o[tq:]
    o_ref[0] = (_rms(o, gd_ref[...]) * (1.0 - LAM_INIT)).astype(BF16)


def prompt_attention(lam, q2, kb, vb, band, gd, batch, seq, tq):
    nq = seq // tq
    head = 2 * DH
    return pl.pallas_call(
        functools.partial(_attn_kernel, tq=tq),
        grid=(batch, H_DIFF, nq),
        in_specs=[pl.BlockSpec(memory_space=pltpu.SMEM),
                  pl.BlockSpec((2, tq, head), lambda b, h, i: (0, b * nq + i, h)),
                  pl.BlockSpec((1, seq, head), lambda b, h, i: (b, 0, h)),
                  pl.BlockSpec((1, seq, head), lambda b, h, i: (b, 0, h)),
                  pl.BlockSpec((1, 2, tq, tq), lambda b, h, i: (h, 0, 0, 0)),
                  _const_spec((1, head))],
        out_specs=pl.BlockSpec((1, tq, head), lambda b, h, i: (b, i, h)),
        out_shape=jax.ShapeDtypeStruct((batch, seq, D_DIFF), BF16),
        scratch_shapes=[pltpu.VMEM((2 * tq, 1), F32), pltpu.VMEM((2 * tq, 1), F32), pltpu.VMEM((2 * tq, head), F32)],
        compiler_params=_params("parallel", "parallel", "arbitrary"),
        name="prompt_attention",
    )(lam, q2, kb, vb, band, gd)


DEC_ROWS = 2 * H_DIFF


def _dec_attn_kernel(pt_ref, lam_ref, q_ref, kc_ref, vc_ref, kn_ref, vn_ref, band_ref, gd_ref, o_ref,
                     m_ref, l_ref, acc_ref, *, n_pages, n_q):
    j = pl.program_id(1)
    q = q_ref[0]

    @pl.when(j == 0)
    def _():
        _softmax_init(m_ref, l_ref, acc_ref)

    @pl.when(j < n_pages - 1)
    def _():
        _softmax_tile(q, kc_ref[0].astype(BF16), vc_ref[0].astype(BF16), None, m_ref, l_ref, acc_ref)

    @pl.when(j == n_pages - 1)
    def _():
        _softmax_tile(q, kc_ref[0].astype(BF16), vc_ref[0].astype(BF16), band_ref[0], m_ref, l_ref, acc_ref)

    @pl.when(j == n_pages)
    def _():
        _softmax_tile(q, kn_ref[0], vn_ref[0], band_ref[1], m_ref, l_ref, acc_ref)
        rows = n_q * DEC_ROWS
        r = lax.broadcasted_iota(I32, (rows, D_DIFF), 0)
        lane = lax.broadcasted_iota(I32, (rows, D_DIFF), 1)
        own = (lane // (2 * DH)) == ((r % DEC_ROWS) // 2)
        w = jnp.where(r % 2 == 0, 1.0, -lam_ref[0])
        o = jnp.where(own, acc_ref[...] / l_ref[...] * w, 0.0)
        o = jnp.sum(o.reshape(n_q, DEC_ROWS, D_DIFF), axis=1)
        outs = []
        for h in range(H_DIFF):
            oh = o[:, h * 2 * DH:(h + 1) * 2 * DH]
            outs.append(_rms(oh, gd_ref[...]) * (1.0 - LAM_INIT))
        o_ref[0] = jnp.concatenate(outs, axis=1).astype(BF16)


def sample_attention(page_table, lam, qst, cache_k, cache_v, kn, vn, band, gd, n_q):
    b, n_pages = page_table.shape
    page = cache_k.shape[1]
    rows = n_q * DEC_ROWS
    cache_spec = pl.BlockSpec((1, page, D_DIFF), lambda i, j, pt: (pt[i, jnp.minimum(j, n_pages - 1)], 0, 0))
    new_spec = pl.BlockSpec((1, page, D_DIFF), lambda i, j, pt: (i, 0, 0))
    grid_spec = pltpu.PrefetchScalarGridSpec(
        num_scalar_prefetch=1,
        grid=(b, n_pages + 1),
        in_specs=[pl.BlockSpec(memory_space=pltpu.SMEM),
                  pl.BlockSpec((1, rows, D_DIFF), lambda i, j, pt: (i, 0, 0)),
                  cache_spec, cache_spec, new_spec, new_spec,
                  pl.BlockSpec((2, rows, page), lambda i, j, pt: (0, 0, 0)),
                  pl.BlockSpec((1, 2 * DH), lambda i, j, pt: (0, 0))],
        out_specs=pl.BlockSpec((1, n_q, D_DIFF), lambda i, j, pt: (i, 0, 0)),
        scratch_shapes=[pltpu.VMEM((rows, 1), F32), pltpu.VMEM((rows, 1), F32), pltpu.VMEM((rows, D_DIFF), F32)],
    )
    return pl.pallas_call(
        functools.partial(_dec_attn_kernel, n_pages=n_pages, n_q=n_q),
        grid_spec=grid_spec,
        out_shape=jax.ShapeDtypeStruct((b, n_q, D_DIFF), BF16),
        compiler_params=_params("parallel", "arbitrary"),
        name="sample_attention",
    )(page_table, lam, qst, cache_k, cache_v, kn, vn, band, gd)


def _mid_kernel(x_ref, ys_ref, yd_ref, wa_ref, wb_ref, g_ref, wq_ref, h_ref, m_ref, qp_ref):
    h = x_ref[...] + _dot(ys_ref[...], wa_ref[...]) + _dot(yd_ref[...], wb_ref[...])
    h_ref[...] = h
    m = _rms(h, g_ref[...]).astype(BF16)
    m_ref[...] = m.astype(F32)
    qp_ref[...] = _dot(m, wq_ref[...]).astype(BF16)


def mid(x, ys, yd, wa, wb, g, wq):
    t = x.shape[0]
    tm = min(ROW_TILE, t)
    row = lambda n: pl.BlockSpec((tm, n), lambda i: (i, 0))
    nq = wq.shape[1]
    return pl.pallas_call(
        _mid_kernel,
        grid=(t // tm,),
        in_specs=[row(D_MODEL), row(D_SSM), row(D_DIFF), _const_spec(wa.shape), _const_spec(wb.shape),
                  _const_spec((1, D_MODEL)), _const_spec(wq.shape)],
        out_specs=(row(D_MODEL), row(D_MODEL), row(nq)),
        out_shape=(jax.ShapeDtypeStruct((t, D_MODEL), F32), jax.ShapeDtypeStruct((t, D_MODEL), F32),
                   jax.ShapeDtypeStruct((t, nq), BF16)),
        compiler_params=_params("parallel"),
        name="mid",
    )(x, ys, yd, wa, wb, g, wq)


def _top16(vals):
    rows = lax.broadcasted_iota(I32, vals.shape, 0)
    big = jnp.int32(vals.shape[0])
    out_v, out_i = [], []
    cur = vals
    for _ in range(PEER_TOPK):
        mx = jnp.max(cur, axis=0, keepdims=True)
        at = jnp.min(jnp.where(cur == mx, rows, big), axis=0, keepdims=True)
        out_v.append(mx)
        out_i.append(at)
        cur = jnp.where(rows == at, -jnp.inf, cur)
    return jnp.concatenate(out_v, axis=0), jnp.concatenate(out_i, axis=0)


HI_MASK = -65536
LOW_HALF_SHIFT = 16


def _topk_kernel(qp_ref, sk_ref, row_ref, shift_ref, gate_ref):
    tb = qp_ref.shape[0]
    half = D_KEY // 2
    idx_rows, gate_rows = [], []
    for h in range(PEER_HEADS):
        s, ix = [], []
        for c in range(2):
            qh = qp_ref[:, (2 * h + c) * half:(2 * h + c + 1) * half]
            sc = _dot_nt(sk_ref[c], qh)
            v, i = _top16(sc)
            s.append(v)
            ix.append(i)
        pieces = [(a, PEER_TOPK if a == 0 else SUBLANES) for a in range(SUBLANES)]
        cand = jnp.concatenate([s[0][a:a + 1, :] + s[1][:nb] for a, nb in pieces] + [s[0][SUBLANES:] + s[1][0:1]],
                               axis=0)
        cidx = jnp.concatenate([ix[0][a:a + 1, :] * N_KEYS + ix[1][:nb] for a, nb in pieces]
                               + [ix[0][SUBLANES:] * N_KEYS + ix[1][0:1]], axis=0)
        sc, pos = _top16(cand)
        prow = lax.broadcasted_iota(I32, cand.shape, 0)
        idx = jnp.concatenate(
            [jnp.max(jnp.where(prow == pos[k:k + 1, :], cidx, -1), axis=0, keepdims=True) for k in range(PEER_TOPK)],
            axis=0)
        e = jnp.exp(sc - sc[0:1, :])
        gate_rows.append(e / jnp.sum(e, axis=0, keepdims=True))
        idx_rows.append(idx)
    idx = jnp.concatenate(idx_rows, axis=0).T
    gate_ref[...] = jnp.concatenate(gate_rows, axis=0).T
    row_ref[...] = (idx >> 1) * SUBLANES
    shift_ref[...] = ((idx & 1) ^ 1) * LOW_HALF_SHIFT


def peer_topk(qp, sk):
    t = qp.shape[0]
    tb = min(PEER_TILE, t)
    out = pl.BlockSpec((tb, SLOTS), lambda i: (i, 0))
    return pl.pallas_call(
        _topk_kernel,
        grid=(t // tb,),
        in_specs=[pl.BlockSpec((tb, qp.shape[1]), lambda i: (i, 0)), _const_spec(sk.shape)],
        out_specs=(out, out, out),
        out_shape=(jax.ShapeDtypeStruct((t, SLOTS), I32), jax.ShapeDtypeStruct((t, SLOTS), I32),
                   jax.ShapeDtypeStruct((t, SLOTS), F32)),
        compiler_params=_params("parallel"),
        name="peer_topk",
    )(qp, sk)


def _pack_kernel(w_ref, o_ref):
    n = o_ref.shape[0]
    even = w_ref[pl.ds(0, n, stride=2), :].astype(BF16).astype(F32)
    odd = w_ref[pl.ds(1, n, stride=2), :].astype(BF16).astype(F32)
    lo = (lax.bitcast_convert_type(even, I32) >> 16) & 0xFFFF
    o_ref[...] = lo | (lax.bitcast_convert_type(odd, I32) & HI_MASK)


def pack_table(w):
    n, d = w.shape
    rows = 2048
    packed = pl.pallas_call(
        _pack_kernel,
        grid=(n // (2 * rows), d // LANES),
        in_specs=[pl.BlockSpec((2 * rows, LANES), lambda i, c: (i, c))],
        out_specs=pl.BlockSpec((rows, LANES), lambda i, c: (i, c)),
        out_shape=jax.ShapeDtypeStruct((n // 2, d), I32),
        compiler_params=_params("parallel", "parallel"),
        name="pack_table",
    )(w)
    return packed.reshape(n // 2 * (d // LANES), LANES)


def _expert_row(tab_ref, row, shift):
    word = tab_ref[pl.ds(pl.multiple_of(row, SUBLANES), SUBLANES), :]
    return lax.bitcast_convert_type((word << shift) & HI_MASK, F32)


UNIT = SUBLANES


def _butterfly(parts, add, roll, select):
    step = SUBLANES // 2
    while len(parts) > 1:
        parts = [select(step, add(a, roll(a, step)), add(b, roll(b, SUBLANES - step)))
                 for a, b in zip(parts[0::2], parts[1::2])]
        step //= 2
    return parts[0]


def _butterfly_order():
    tiles = [[{p}] * SUBLANES for p in range(SUBLANES)]
    out = _butterfly(
        tiles,
        add=lambda a, b: [x | y for x, y in zip(a, b)],
        roll=lambda a, k: [a[(i - k) % SUBLANES] for i in range(SUBLANES)],
        select=lambda step, a, b: [a[i] if (i // step) % 2 else b[i] for i in range(SUBLANES)])
    order = [0] * SUBLANES
    for sublane, owners in enumerate(out):
        (p,) = owners
        order[p] = sublane
    return order


BUTTERFLY_ORDER = _butterfly_order()


def _rows_to_sublanes(parts):
    sub = lax.broadcasted_iota(I32, (SUBLANES, LANES), 0)
    placed = [None] * SUBLANES
    for i, part in enumerate(parts):
        placed[BUTTERFLY_ORDER.index(i)] = part
    return _butterfly(placed, add=lambda a, b: a + b, roll=lambda a, k: pltpu.roll(a, k, 0),
                      select=lambda step, a, b: jnp.where((sub // step) % 2 == 1, a, b))


def _peer_u_kernel(row_ref, shift_ref, m_ref, gate_ref, shiftv_ref, tab_ref, wcode_ref, part_ref, h_ref):
    tb = m_ref.shape[0]
    for t in range(tb):
        x = m_ref[t]
        for u in range(SLOTS // UNIT):
            parts = [_expert_row(tab_ref, row_ref[t, u * UNIT + e], shift_ref[t, u * UNIT + e]) * x
                     for e in range(UNIT)]
            part_ref[t, u * UNIT:(u + 1) * UNIT, :] = _rows_to_sublanes(parts)
    for t in range(tb):
        h_ref[t:t + 1, :] = jnp.sum(part_ref[t].T, axis=0, keepdims=True)
    h = h_ref[...]
    act = 0.5 * h * (1.0 + lax.erf(h * (2.0 ** -0.5)))
    w = (gate_ref[...] * act).astype(BF16).astype(F32)
    wcode_ref[...] = lax.bitcast_convert_type(w, I32) | shiftv_ref[...]


def peer_u(row, shift, m3, gate, tab):
    t = m3.shape[0]
    tb = min(EXPERT_TILE, t)
    smem = pl.BlockSpec((tb, SLOTS), lambda i: (i, 0), memory_space=pltpu.SMEM)
    vmem = pl.BlockSpec((tb, SLOTS), lambda i: (i, 0))
    return pl.pallas_call(
        _peer_u_kernel,
        grid=(t // tb,),
        in_specs=[smem, smem, pl.BlockSpec((tb, SUBLANES, LANES), lambda i: (i, 0, 0)), vmem, vmem,
                  pl.BlockSpec(tab.shape, lambda i: (0, 0), pipeline_mode=pl.Buffered(1))],
        out_specs=vmem,
        out_shape=jax.ShapeDtypeStruct((t, SLOTS), I32),
        scratch_shapes=[pltpu.VMEM((tb, SLOTS, LANES), F32), pltpu.VMEM((tb, SLOTS), F32)],
        compiler_params=_params("arbitrary"),
        name="peer_u",
    )(row, shift, m3, gate, shift, tab)


N_ACC = 4


def _peer_v_kernel(row_ref, wcode_ref, tab_ref, o_ref):
    tb = o_ref.shape[0]
    for t in range(tb):
        acc = [None] * N_ACC
        for j in range(SLOTS):
            code = jnp.full((SUBLANES, LANES), wcode_ref[t, j], I32)
            weight = lax.bitcast_convert_type(code & HI_MASK, F32)
            term = weight * _expert_row(tab_ref, row_ref[t, j], code & LOW_HALF_SHIFT)
            acc[j % N_ACC] = term if acc[j % N_ACC] is None else acc[j % N_ACC] + term
        o_ref[t] = (acc[0] + acc[1]) + (acc[2] + acc[3])


def peer_v(row, wcode, tab):
    t = row.shape[0]
    tb = min(EXPERT_TILE, t)
    smem = pl.BlockSpec((tb, SLOTS), lambda i: (i, 0), memory_space=pltpu.SMEM)
    return pl.pallas_call(
        _peer_v_kernel,
        grid=(t // tb,),
        in_specs=[smem, smem, pl.BlockSpec(tab.shape, lambda i: (0, 0), pipeline_mode=pl.Buffered(1))],
        out_specs=pl.BlockSpec((tb, SUBLANES, LANES), lambda i: (i, 0, 0)),
        out_shape=jax.ShapeDtypeStruct((t, SUBLANES, LANES), F32),
        compiler_params=_params("arbitrary"),
        name="peer_v",
    )(row, wcode, tab)


def _ple_kernel(h_ref, peer_ref, p_ref, wp_ref, wg_ref, gp_ref, gf_ref, y_ref):
    h = h_ref[...] + peer_ref[...]
    gate = jax.nn.sigmoid(_dot(_rms(h, gp_ref[...]).astype(BF16), wg_ref[...]))
    h = h + _dot(p_ref[...].astype(BF16), wp_ref[...]) * gate
    y_ref[...] = _rms(h, gf_ref[...])


def ple(h, peer_out, p, wp, wg, gp, gf):
    t = h.shape[0]
    tm = min(ROW_TILE, t)
    row = lambda n: pl.BlockSpec((tm, n), lambda i: (i, 0))
    return pl.pallas_call(
        _ple_kernel,
        grid=(t // tm,),
        in_specs=[row(D_MODEL), row(D_MODEL), row(PLE_DIM), _const_spec(wp.shape), _const_spec(wg.shape),
                  _const_spec((1, D_MODEL)), _const_spec((1, D_MODEL))],
        out_specs=row(D_MODEL),
        out_shape=jax.ShapeDtypeStruct((t, D_MODEL), F32),
        compiler_params=_params("parallel"),
        name="ple",
    )(h, peer_out, p, wp, wg, gp, gf)


def _t5_bucket(rel):
    n = jnp.maximum(rel, 0)
    max_exact = N_BUCKETS // 2
    nf = jnp.maximum(n, 1).astype(F32)
    large = max_exact + (jnp.log(nf / max_exact) / math.log(REL_MAX_DIST / max_exact)
                         * (N_BUCKETS - max_exact)).astype(I32)
    large = jnp.minimum(large, N_BUCKETS - 1)
    return jnp.where(n < max_exact, n, large)


def _band_bias(rel_bias, rel):
    far = rel_bias[N_BUCKETS - 1]
    bucket = _t5_bucket(rel)[..., None]
    b = jnp.zeros(rel.shape + (H_DIFF,), F32)
    for n in range(N_BUCKETS - 1):
        b = jnp.where(bucket == n, rel_bias[n] - far, b)
    b = jnp.where((rel >= 0)[..., None], b, NEG_INF)
    return jnp.moveaxis(b, -1, 0).astype(F32)


def _pad_rows(a, rows):
    return jnp.pad(a, ((0, 0), (0, rows - a.shape[1]), (0, 0)))


def _layer(x, p_l, conv_buf, ssm0, lam, w, attention):
    b, l, _ = x.shape
    t = b * l
    x2 = x.reshape(t, D_MODEL)
    z, xbc, q2, k, v, kb, vb, dt = in_proj(x2, w["g_mix"], w["wz"], w["wx"], w["wq"], w["wk"], w["wv"], w["wdt"])

    lp = -(-l // SSD_CHUNK) * SSD_CHUNK
    xbc3 = xbc.reshape(b, l, CONV_DIM)
    cbuf = jnp.pad(conv_buf, ((0, 0), (SUBLANES - (CONV_W - 1), 0), (0, 0)))
    y_ssm, ssm_final = ssd(_pad_rows(xbc3, lp), _pad_rows(z.reshape(b, l, D_SSM), lp),
                           _pad_rows(dt.reshape(b, l, LANES), lp), cbuf, ssm0,
                           w["conv_w"], w["conv_b"], w["dt_bias"], w["a_log"], w["d_skip"], w["g_ssm_norm"], l)
    y_ssm = y_ssm[:, :l].reshape(t, D_SSM)
    conv_new = jnp.concatenate([conv_buf, xbc3], axis=1)[:, -(CONV_W - 1):]

    y_diff = attention(q2, kb, vb)

    h1, m, qp = mid(x2, y_ssm, y_diff, w["wo_a"], w["wo_b"], w["g_ffn"], w["w_query"])
    row, shift, gate = peer_topk(qp, w["sub_keys"])
    wcode = peer_u(row, shift, m.reshape(t, SUBLANES, LANES), gate, w["tab_u"])
    peer_out = peer_v(row, wcode, w["tab_v"]).reshape(t, D_MODEL)
    y = ple(h1, peer_out, p_l.reshape(t, PLE_DIM), w["w_ple"], w["w_ple_gate"], w["g_ple"], w["g_final"])
    return (y.reshape(b, l, D_MODEL), k.reshape(b, l, H_DIFF, 2, DH), v.reshape(b, l, H_DIFF, 2 * DH),
            ssm_final, conv_new)


def kernel(x_prompt, x_sample, cache_k, cache_v, state_ssm, state_conv, page_table, p_prompt, p_sample, g_mix, w_in, conv_w, conv_b, dt_bias, a_log, d_skip, g_ssm_norm, lam_q1, lam_k1, lam_q2, lam_k2, g_diff_norm, rel_bias, w_out, g_ffn, w_query, sub_keys, expert_u, expert_v, w_ple, w_ple_gate, g_ple, g_final):
    assert w_in.shape[0] == 1, "single-layer problem"
    bsz, seq, _ = x_prompt.shape
    dec_b, dec_l, _ = x_sample.shape
    n_pages = page_table.shape[1]
    page = cache_k.shape[2]
    past = n_pages * page

    wi = w_in[0].astype(BF16)
    o_x = D_SSM
    o_dt = o_x + CONV_DIM
    o_q = o_dt + H_SSM
    o_k = o_q + D_DIFF
    o_v = o_k + D_DIFF
    lanes_of = lambda a: jnp.pad(a.astype(F32)[None, :], ((0, 0), (0, LANES - a.shape[0])))
    w = dict(
        g_mix=g_mix[0][None], wz=wi[:, :o_x], wx=wi[:, o_x:o_dt],
        wdt=jnp.pad(wi[:, o_dt:o_q], ((0, 0), (0, LANES - H_SSM))),
        wq=wi[:, o_q:o_k], wk=wi[:, o_k:o_v], wv=wi[:, o_v:],
        conv_w=conv_w[0], conv_b=conv_b[0][None], dt_bias=lanes_of(dt_bias[0]), a_log=lanes_of(a_log[0]),
        d_skip=lanes_of(d_skip[0]), g_ssm_norm=g_ssm_norm[0][None],
        wo_a=w_out[0][:D_SSM].astype(BF16), wo_b=w_out[0][D_SSM:].astype(BF16), g_ffn=g_ffn[0][None],
        w_query=w_query[0].astype(BF16), sub_keys=sub_keys[0].astype(BF16),
        tab_u=pack_table(expert_u[0]), tab_v=pack_table(expert_v[0]),
        w_ple=w_ple[0].astype(BF16), w_ple_gate=w_ple_gate[0].astype(BF16), g_ple=g_ple[0][None],
        g_final=g_final[None],
    )
    gd = g_diff_norm[0][None]
    lam = (jnp.exp(jnp.sum(lam_q1[0] * lam_k1[0])) - jnp.exp(jnp.sum(lam_q2[0] * lam_k2[0])) + LAM_INIT).reshape(1)

    tq = min(ATTN_TILE, seq)
    ii = jnp.arange(tq, dtype=I32)
    rel = jnp.stack([ii[:, None] - ii[None, :], tq + ii[:, None] - ii[None, :]])
    band_p = _band_bias(rel_bias, rel)

    def attn_prompt(q2, kb, vb):
        o = prompt_attention(lam, q2, kb.reshape(bsz, seq, D_DIFF), vb.reshape(bsz, seq, D_DIFF), band_p, gd,
                             bsz, seq, tq)
        return o.reshape(bsz * seq, D_DIFF)

    yp, kp, vp, sp, cp = _layer(x_prompt, p_prompt[0], jnp.zeros((bsz, CONV_W - 1, CONV_DIM), F32),
                                jnp.zeros((bsz, H_SSM, P_SSM, N_STATE), F32), lam, w, attn_prompt)

    rows = dec_l * DEC_ROWS
    qpos = past + jnp.arange(rows, dtype=I32) // DEC_ROWS
    head_of = (jnp.arange(rows, dtype=I32) % DEC_ROWS) // 2
    kk = jnp.arange(page, dtype=I32)
    rel_last = qpos[:, None] - (past - page + kk)[None, :]
    rel_new = jnp.where(kk[None, :] < dec_l, qpos[:, None] - (past + kk)[None, :], -1)
    band_all = _band_bias(rel_bias, jnp.stack([rel_last, rel_new]))
    band_s = jnp.take_along_axis(band_all, head_of[None, None, :, None], axis=0)[0]
    lane_slot = jnp.arange(D_DIFF, dtype=I32) // DH
    row_slot = jnp.arange(rows, dtype=I32) % DEC_ROWS
    q_mask = (lane_slot[None, :] == row_slot[:, None])

    def attn_sample(q2, kb, vb):
        q = (q2[0] + q2[1]).reshape(dec_b, dec_l, D_DIFF)
        qst = jnp.where(q_mask[None], jnp.repeat(q, DEC_ROWS, axis=1), jnp.zeros((), BF16))
        kn = _pad_rows(kb.reshape(dec_b, dec_l, D_DIFF), page)
        vn = _pad_rows(vb.reshape(dec_b, dec_l, D_DIFF), page)
        o = sample_attention(page_table, lam, qst, cache_k[0].reshape(-1, page, D_DIFF),
                             cache_v[0].reshape(-1, page, D_DIFF), kn, vn, band_s, gd, dec_l)
        return o.reshape(dec_b * dec_l, D_DIFF)

    ys, ks, vs, ss, cs = _layer(x_sample, p_sample[0], state_conv[0], state_ssm[0], lam, w, attn_sample)

    return (yp, ys, kp[None], vp[None], sp[None], cp[None], ks[None], vs[None], ss[None], cs[None])
```

```python
import functools
import math

import jax
import jax.numpy as jnp
from jax import lax
from jax.experimental import pallas as pl
from jax.experimental.pallas import tpu as pltpu

F32 = jnp.float32
BF16 = jnp.bfloat16
I32 = jnp.int32

D_MODEL = 1024
D_SSM = 512
D_DIFF = 512
P_SSM = 64
H_SSM = 8
N_STATE = 128
SSM_GROUPS = 2
HEADS_PER_GROUP = H_SSM // SSM_GROUPS
CONV_W = 4
CONV_DIM = D_SSM + 2 * SSM_GROUPS * N_STATE
DH = 64
H_DIFF = 4
N_BUCKETS = 32
REL_MAX_DIST = 128
N_KEYS = 128
PEER_HEADS = 8
PEER_TOPK = 16
D_KEY = 256
PLE_DIM = 256
NEG_INF = -1e30
RMS_EPS = 1e-6
LAM_INIT = 0.8 - 0.6 * math.exp(-0.3 * 0)

LANES = 128
SUBLANES = 8
VMEM_LIMIT = 56 * 1024 * 1024

SSD_CHUNK = 128
ROW_TILE = 256
ATTN_TILE = 512
PEER_TILE = 128
EXPERT_TILE = 16
SLOTS = PEER_HEADS * PEER_TOPK


def _params(*sem):
    return pltpu.CompilerParams(dimension_semantics=sem, vmem_limit_bytes=VMEM_LIMIT)


def _const_spec(shape):
    nd = len(shape)
    return pl.BlockSpec(shape, lambda *_: (0,) * nd)


def _rms(x, g):
    return x * lax.rsqrt(jnp.mean(x * x, axis=-1, keepdims=True) + RMS_EPS) * g


def _dot(a, b):
    return jnp.dot(a, b, preferred_element_type=F32)


def _dot_nt(a, b):
    return lax.dot_general(a, b, (((1,), (1,)), ((), ())), preferred_element_type=F32)


def _dot_tn(a, b):
    return lax.dot_general(a, b, (((0,), (0,)), ((), ())), preferred_element_type=F32)


def _in_proj_kernel(x_ref, g_ref, wz_ref, wx_ref, wq_ref, wk_ref, wv_ref, wdt_ref,
                    z_ref, xbc_ref, q2_ref, k_ref, v_ref, kb_ref, vb_ref, dt_ref):
    a = _rms(x_ref[...], g_ref[...]).astype(BF16)
    z_ref[...] = _dot(a, wz_ref[...])
    xbc_ref[...] = _dot(a, wx_ref[...])
    dt_ref[...] = _dot(a, wdt_ref[...])
    k = _dot(a, wk_ref[...])
    v = _dot(a, wv_ref[...])
    k_ref[...] = k
    v_ref[...] = v
    kb_ref[...] = k.astype(BF16)
    vb_ref[...] = v.astype(BF16)
    q = _dot(a, wq_ref[...]).astype(BF16) * (DH ** -0.5)
    lane = lax.broadcasted_iota(I32, q.shape, 1)
    first_map = (lane % (2 * DH)) < DH
    zero = jnp.zeros_like(q)
    q2_ref[0] = jnp.where(first_map, q, zero)
    q2_ref[1] = jnp.where(first_map, zero, q)


def in_proj(x, g, wz, wx, wq, wk, wv, wdt):
    t = x.shape[0]
    tm = min(ROW_TILE, t)
    row = lambda n: pl.BlockSpec((tm, n), lambda i: (i, 0))
    out_shape = (
        jax.ShapeDtypeStruct((t, D_SSM), F32),
        jax.ShapeDtypeStruct((t, CONV_DIM), F32),
        jax.ShapeDtypeStruct((2, t, D_DIFF), BF16),
        jax.ShapeDtypeStruct((t, D_DIFF), F32),
        jax.ShapeDtypeStruct((t, D_DIFF), F32),
        jax.ShapeDtypeStruct((t, D_DIFF), BF16),
        jax.ShapeDtypeStruct((t, D_DIFF), BF16),
        jax.ShapeDtypeStruct((t, LANES), F32),
    )
    return pl.pallas_call(
        _in_proj_kernel,
        grid=(t // tm,),
        in_specs=[row(D_MODEL), _const_spec((1, D_MODEL)), _const_spec(wz.shape), _const_spec(wx.shape),
                  _const_spec(wq.shape), _const_spec(wk.shape), _const_spec(wv.shape), _const_spec(wdt.shape)],
        out_specs=(row(D_SSM), row(CONV_DIM), pl.BlockSpec((2, tm, D_DIFF), lambda i: (0, i, 0)),
                   row(D_DIFF), row(D_DIFF), row(D_DIFF), row(D_DIFF), row(LANES)),
        out_shape=out_shape,
        compiler_params=_params("parallel"),
        name="in_proj",
    )(x, g, wz, wx, wq, wk, wv, wdt)


def _ssd_kernel(xbc_ref, z_ref, dt_ref, cbuf_ref, s0_ref, cw_ref, cb_ref, dtb_ref, alog_ref, dskip_ref, gn_ref,
                y_ref, sfin_ref, win_ref, state_ref, *, seq_len):
    c = pl.program_id(1)
    lc = SSD_CHUNK

    @pl.when(c == 0)
    def _():
        win_ref[0:SUBLANES, :] = cbuf_ref[0]
        state_ref[...] = s0_ref[0]

    win_ref[SUBLANES:SUBLANES + lc, :] = xbc_ref[0]
    base = SUBLANES - (CONV_W - 1)
    conv = cb_ref[...]
    for j in range(CONV_W):
        conv = conv + cw_ref[j:j + 1, :] * win_ref[base + j:base + j + lc, :]
    win_ref[0:SUBLANES, :] = win_ref[lc:lc + SUBLANES, :]
    xc = conv * jax.nn.sigmoid(conv)

    pos = c * lc + lax.broadcasted_iota(I32, (lc, LANES), 0)
    dt = jax.nn.softplus(dt_ref[0] + dtb_ref[...])
    dt = jnp.where(pos < seq_len, dt, 0.0)
    adt = -jnp.exp(alog_ref[...]) * dt
    row = lax.broadcasted_iota(I32, (lc, lc), 0)
    col = lax.broadcasted_iota(I32, (lc, lc), 1)
    causal = row >= col
    tri = causal.astype(F32)
    cs = jnp.dot(tri, adt, preferred_element_type=F32, precision=lax.Precision.HIGHEST)
    cs_t = cs.T
    cs_last = cs[lc - 1:lc, :]

    ys = []
    for g in range(SSM_GROUPS):
        b0 = D_SSM + g * N_STATE
        c0 = D_SSM + SSM_GROUPS * N_STATE + g * N_STATE
        bg = xc[:, b0:b0 + N_STATE]
        cg = xc[:, c0:c0 + N_STATE].astype(BF16)
        cb = _dot_nt(cg, bg.astype(BF16))
        for hh in range(HEADS_PER_GROUP):
            h = g * HEADS_PER_GROUP + hh
            xs = xc[:, h * P_SSM:(h + 1) * P_SSM]
            cs_h = cs[:, h:h + 1]
            decay = jnp.where(causal, jnp.exp(jnp.where(causal, cs_h - cs_t[h:h + 1, :], 0.0)), 0.0)
            xdt = xs * dt[:, h:h + 1]
            y = _dot((cb * decay).astype(BF16), xdt.astype(BF16))
            st = state_ref[h]
            y = y + jnp.exp(cs_h) * _dot_nt(cg, st.astype(BF16))
            y = y + dskip_ref[:, h:h + 1] * xs
            ys.append(y)
            to_end = jnp.exp(cs_last[:, h:h + 1] - cs_h)
            new = _dot_tn((xdt * to_end).astype(BF16), bg.astype(BF16))
            state_ref[h] = jnp.exp(cs_last[:, h:h + 1]) * st + new
    y = jnp.concatenate(ys, axis=1)
    zz = z_ref[0]
    y = y * (zz * jax.nn.sigmoid(zz))
    gw = D_SSM // SSM_GROUPS
    outs = []
    for g in range(SSM_GROUPS):
        yg = y[:, g * gw:(g + 1) * gw]
        outs.append(yg * lax.rsqrt(jnp.mean(yg * yg, axis=-1, keepdims=True) + RMS_EPS))
    y_ref[0] = (jnp.concatenate(outs, axis=1) * gn_ref[...]).astype(BF16)

    @pl.when(c == pl.num_programs(1) - 1)
    def _():
        sfin_ref[0] = state_ref[...]


def ssd(xbc, z, dt, cbuf, s0, cw, cb, dtb, alog, dskip, gn, seq_len):
    b, lp, _ = xbc.shape
    lc = SSD_CHUNK
    seq = lambda n: pl.BlockSpec((1, lc, n), lambda i, c: (i, c, 0))
    return pl.pallas_call(
        functools.partial(_ssd_kernel, seq_len=seq_len),
        grid=(b, lp // lc),
        in_specs=[seq(CONV_DIM), seq(D_SSM), seq(LANES),
                  pl.BlockSpec((1, SUBLANES, CONV_DIM), lambda i, c: (i, 0, 0)),
                  pl.BlockSpec((1, H_SSM, P_SSM, N_STATE), lambda i, c: (i, 0, 0, 0)),
                  _const_spec(cw.shape), _const_spec(cb.shape), _const_spec(dtb.shape), _const_spec(alog.shape),
                  _const_spec(dskip.shape), _const_spec(gn.shape)],
        out_specs=(seq(D_SSM), pl.BlockSpec((1, H_SSM, P_SSM, N_STATE), lambda i, c: (i, 0, 0, 0))),
        out_shape=(jax.ShapeDtypeStruct((b, lp, D_SSM), BF16),
                   jax.ShapeDtypeStruct((b, H_SSM, P_SSM, N_STATE), F32)),
        scratch_shapes=[pltpu.VMEM((SUBLANES + lc, CONV_DIM), F32), pltpu.VMEM((H_SSM, P_SSM, N_STATE), F32)],
        compiler_params=_params("parallel", "arbitrary"),
        name="ssd",
    )(xbc, z, dt, cbuf, s0, cw, cb, dtb, alog, dskip, gn)


def _softmax_tile(q, k, v, bias, m_ref, l_ref, acc_ref):
    s = _dot_nt(q, k)
    if bias is not None:
        s = s + bias
    m_old = m_ref[...]
    m_new = jnp.maximum(m_old, jnp.max(s, axis=-1, keepdims=True))
    alpha = jnp.exp(m_old - m_new)
    p = jnp.exp(s - m_new)
    l_ref[...] = alpha * l_ref[...] + jnp.sum(p, axis=-1, keepdims=True)
    acc_ref[...] = alpha * acc_ref[...] + _dot(p.astype(BF16), v)
    m_ref[...] = m_new


def _softmax_init(m_ref, l_ref, acc_ref):
    m_ref[...] = jnp.full(m_ref.shape, NEG_INF, F32)
    l_ref[...] = jnp.zeros(l_ref.shape, F32)
    acc_ref[...] = jnp.zeros(acc_ref.shape, F32)


def _attn_kernel(lam_ref, qt_ref, k_ref, vt_ref, band_ref, gd_ref, o_ref, m_ref, l_ref, acc_ref, *, tq):
    i = pl.program_id(2)
    qt = qt_ref[0, 0, 0]
    _softmax_init(m_ref, l_ref, acc_ref)

    def tile(j, bias):
        s = _dot(k_ref[0, j], qt)
        if bias is not None:
            s = s + jnp.concatenate([bias, bias], axis=1)
        m_old = m_ref[...]
        m_new = jnp.maximum(m_old, jnp.max(s, axis=0, keepdims=True))
        alpha = jnp.exp(m_old - m_new)
        p = jnp.exp(s - m_new)
        l_ref[...] = alpha * l_ref[...] + jnp.sum(p, axis=0, keepdims=True)
        acc_ref[...] = alpha * acc_ref[...] + _dot(vt_ref[0, 0, j], p.astype(BF16))
        m_ref[...] = m_new

    def far(j, carry):
        tile(j, None)
        return carry

    lax.fori_loop(0, jnp.maximum(i - 1, 0), far, 0)

    @pl.when(i >= 1)
    def _():
        tile(i - 1, band_ref[0, 1])

    tile(i, band_ref[0, 0])

    o = acc_ref[...] / l_ref[...]
    o = o[:, :tq] - lam_ref[0] * o[:, tq:]
    o = o * lax.rsqrt(jnp.mean(o * o, axis=0, keepdims=True) + RMS_EPS) * gd_ref[...] * (1.0 - LAM_INIT)
    o_ref[0] = o.T.astype(BF16)


def prompt_attention(lam, q2, kb, vb, band_t, gd, batch, seq, tq):
    nq = seq // tq
    head = 2 * DH
    qt = jnp.transpose(q2.reshape(2, batch, nq, tq, H_DIFF, head), (1, 4, 2, 5, 0, 3)).reshape(
        batch, H_DIFF, nq, head, 2 * tq)
    k4 = kb.reshape(batch, nq, tq, D_DIFF)
    vt = jnp.transpose(vb.reshape(batch, nq, tq, H_DIFF, head), (0, 3, 1, 4, 2))
    return pl.pallas_call(
        functools.partial(_attn_kernel, tq=tq),
        grid=(batch, H_DIFF, nq),
        in_specs=[pl.BlockSpec(memory_space=pltpu.SMEM),
                  pl.BlockSpec((1, 1, 1, head, 2 * tq), lambda b, h, i: (b, h, i, 0, 0)),
                  pl.BlockSpec((1, nq, tq, head), lambda b, h, i: (b, 0, 0, h)),
                  pl.BlockSpec((1, 1, nq, head, tq), lambda b, h, i: (b, h, 0, 0, 0)),
                  pl.BlockSpec((1, 2, tq, tq), lambda b, h, i: (h, 0, 0, 0)),
                  _const_spec((head, 1))],
        out_specs=pl.BlockSpec((1, tq, head), lambda b, h, i: (b, i, h)),
        out_shape=jax.ShapeDtypeStruct((batch, seq, D_DIFF), BF16),
        scratch_shapes=[pltpu.VMEM((1, 2 * tq), F32), pltpu.VMEM((1, 2 * tq), F32), pltpu.VMEM((head, 2 * tq), F32)],
        compiler_params=_params("parallel", "parallel", "arbitrary"),
        name="prompt_attention",
    )(lam, qt, k4, vt, band_t, gd.reshape(head, 1))


DEC_ROWS = 2 * H_DIFF


def _dec_attn_kernel(pt_ref, lam_ref, q_ref, *refs, n_groups, n_q, group):
    k_refs, v_refs = refs[:group], refs[group:2 * group]
    kn_ref, vn_ref, band_ref, gd_ref, o_ref, m_ref, l_ref, acc_ref = refs[2 * group:]
    j = pl.program_id(1)
    q = q_ref[0]
    page = band_ref.shape[2]

    @pl.when(j == 0)
    def _():
        _softmax_init(m_ref, l_ref, acc_ref)

    def update(s, weighted_values):
        m_old = m_ref[...]
        m_new = jnp.maximum(m_old, jnp.max(s, axis=-1, keepdims=True))
        alpha = jnp.exp(m_old - m_new)
        p = jnp.exp(s - m_new)
        l_ref[...] = alpha * l_ref[...] + jnp.sum(p, axis=-1, keepdims=True)
        acc_ref[...] = alpha * acc_ref[...] + weighted_values(p.astype(BF16))
        m_ref[...] = m_new

    scores = [_dot(q, k_ref[0].reshape(D_DIFF, page).astype(BF16)) for k_ref in k_refs]
    scores[-1] = scores[-1] + jnp.where(j == n_groups - 1, band_ref[0], 0.0)

    def cached_values(p):
        heads = []
        for h in range(H_DIFF):
            o = None
            for u, v_ref in enumerate(v_refs):
                part = _dot(p[:, u * page:(u + 1) * page], v_ref[0, :, h, :].astype(BF16))
                o = part if o is None else o + part
            heads.append(o)
        return jnp.concatenate(heads, axis=1)

    update(jnp.concatenate(scores, axis=1), cached_values)

    @pl.when(j == n_groups - 1)
    def _():
        update(_dot_nt(q, kn_ref[0]) + band_ref[1], lambda p: _dot(p, vn_ref[0]))
        rows = n_q * DEC_ROWS
        r = lax.broadcasted_iota(I32, (rows, D_DIFF), 0)
        lane = lax.broadcasted_iota(I32, (rows, D_DIFF), 1)
        own = (lane // (2 * DH)) == ((r % DEC_ROWS) // 2)
        w = jnp.where(r % 2 == 0, 1.0, -lam_ref[0])
        o = jnp.where(own, acc_ref[...] / l_ref[...] * w, 0.0)
        o = jnp.sum(o.reshape(n_q, DEC_ROWS, D_DIFF), axis=1)
        outs = []
        for h in range(H_DIFF):
            oh = o[:, h * 2 * DH:(h + 1) * 2 * DH]
            outs.append(_rms(oh, gd_ref[...]) * (1.0 - LAM_INIT))
        o_ref[0] = jnp.concatenate(outs, axis=1).astype(BF16)


PAGE_GROUP = 8


def sample_attention(page_table, lam, qst, cache_kt, cache_v, kn, vn, band, gd, n_q):
    b, n_pages = page_table.shape
    page = cache_v.shape[1]
    rows = n_q * DEC_ROWS
    group = math.gcd(PAGE_GROUP, n_pages)
    n_groups = n_pages // group

    def page_of(u):
        return lambda i, j, pt: (pt[i, j * group + u],)

    k_specs = [pl.BlockSpec((1,) + cache_kt.shape[1:], lambda i, j, pt, f=page_of(u): f(i, j, pt) + (0, 0, 0, 0))
               for u in range(group)]
    v_specs = [pl.BlockSpec((1,) + cache_v.shape[1:], lambda i, j, pt, f=page_of(u): f(i, j, pt) + (0, 0, 0))
               for u in range(group)]
    new_spec = pl.BlockSpec((1, page, D_DIFF), lambda i, j, pt: (i, 0, 0))
    grid_spec = pltpu.PrefetchScalarGridSpec(
        num_scalar_prefetch=1,
        grid=(b, n_groups),
        in_specs=[pl.BlockSpec(memory_space=pltpu.SMEM),
                  pl.BlockSpec((1, rows, D_DIFF), lambda i, j, pt: (i, 0, 0))]
                 + k_specs + v_specs
                 + [new_spec, new_spec,
                    pl.BlockSpec((2, rows, page), lambda i, j, pt: (0, 0, 0)),
                    pl.BlockSpec((1, 2 * DH), lambda i, j, pt: (0, 0))],
        out_specs=pl.BlockSpec((1, n_q, D_DIFF), lambda i, j, pt: (i, 0, 0)),
        scratch_shapes=[pltpu.VMEM((rows, 1), F32), pltpu.VMEM((rows, 1), F32), pltpu.VMEM((rows, D_DIFF), F32)],
    )
    return pl.pallas_call(
        functools.partial(_dec_attn_kernel, n_groups=n_groups, n_q=n_q, group=group),
        grid_spec=grid_spec,
        out_shape=jax.ShapeDtypeStruct((b, n_q, D_DIFF), BF16),
        compiler_params=_params("parallel", "arbitrary"),
        name="sample_attention",
    )(page_table, lam, qst, *([cache_kt] * group), *([cache_v] * group), kn, vn, band, gd)


def _mid_kernel(x_ref, ys_ref, yd_ref, wa_ref, wb_ref, g_ref, wq_ref, h_ref, m_ref, qp_ref):
    h = x_ref[...] + _dot(ys_ref[...], wa_ref[...]) + _dot(yd_ref[...], wb_ref[...])
    h_ref[...] = h
    m = _rms(h, g_ref[...]).astype(BF16)
    m_ref[...] = m.astype(F32)
    qp_ref[...] = _dot(m, wq_ref[...]).astype(BF16)


def mid(x, ys, yd, wa, wb, g, wq):
    t = x.shape[0]
    tm = min(ROW_TILE, t)
    row = lambda n: pl.BlockSpec((tm, n), lambda i: (i, 0))
    nq = wq.shape[1]
    return pl.pallas_call(
        _mid_kernel,
        grid=(t // tm,),
        in_specs=[row(D_MODEL), row(D_SSM), row(D_DIFF), _const_spec(wa.shape), _const_spec(wb.shape),
                  _const_spec((1, D_MODEL)), _const_spec(wq.shape)],
        out_specs=(row(D_MODEL), row(D_MODEL), row(nq)),
        out_shape=(jax.ShapeDtypeStruct((t, D_MODEL), F32), jax.ShapeDtypeStruct((t, D_MODEL), F32),
                   jax.ShapeDtypeStruct((t, nq), BF16)),
        compiler_params=_params("parallel"),
        name="mid",
    )(x, ys, yd, wa, wb, g, wq)


def _top16(vals):
    rows = lax.broadcasted_iota(I32, vals.shape, 0)
    big = jnp.int32(vals.shape[0])
    out_v, out_i = [], []
    cur = vals
    for _ in range(PEER_TOPK):
        mx = jnp.max(cur, axis=0, keepdims=True)
        at = jnp.min(jnp.where(cur == mx, rows, big), axis=0, keepdims=True)
        out_v.append(mx)
        out_i.append(at)
        cur = jnp.where(rows == at, -jnp.inf, cur)
    return jnp.concatenate(out_v, axis=0), jnp.concatenate(out_i, axis=0)


HI_MASK = -65536
LOW_HALF_SHIFT = 16


def _topk_kernel(qp_ref, sk_ref, row_ref, shift_ref, gate_ref):
    tb = qp_ref.shape[0]
    half = D_KEY // 2
    idx_rows, gate_rows = [], []
    for h in range(PEER_HEADS):
        s, ix = [], []
        for c in range(2):
            qh = qp_ref[:, (2 * h + c) * half:(2 * h + c + 1) * half]
            sc = _dot_nt(sk_ref[c], qh)
            v, i = _top16(sc)
            s.append(v)
            ix.append(i)
        pieces = [(a, PEER_TOPK if a == 0 else SUBLANES) for a in range(SUBLANES)]
        cand = jnp.concatenate([s[0][a:a + 1, :] + s[1][:nb] for a, nb in pieces] + [s[0][SUBLANES:] + s[1][0:1]],
                               axis=0)
        cidx = jnp.concatenate([ix[0][a:a + 1, :] * N_KEYS + ix[1][:nb] for a, nb in pieces]
                               + [ix[0][SUBLANES:] * N_KEYS + ix[1][0:1]], axis=0)
        sc, pos = _top16(cand)
        prow = lax.broadcasted_iota(I32, cand.shape, 0)
        idx = jnp.concatenate(
            [jnp.max(jnp.where(prow == pos[k:k + 1, :], cidx, -1), axis=0, keepdims=True) for k in range(PEER_TOPK)],
            axis=0)
        e = jnp.exp(sc - sc[0:1, :])
        gate_rows.append(e / jnp.sum(e, axis=0, keepdims=True))
        idx_rows.append(idx)
    idx = jnp.concatenate(idx_rows, axis=0).T
    gate_ref[...] = jnp.concatenate(gate_rows, axis=0).T
    row_ref[...] = (idx >> 1) * SUBLANES
    shift_ref[...] = ((idx & 1) ^ 1) * LOW_HALF_SHIFT


def peer_topk(qp, sk):
    t = qp.shape[0]
    tb = min(PEER_TILE, t)
    out = pl.BlockSpec((tb, SLOTS), lambda i: (i, 0))
    return pl.pallas_call(
        _topk_kernel,
        grid=(t // tb,),
        in_specs=[pl.BlockSpec((tb, qp.shape[1]), lambda i: (i, 0)), _const_spec(sk.shape)],
        out_specs=(out, out, out),
        out_shape=(jax.ShapeDtypeStruct((t, SLOTS), I32), jax.ShapeDtypeStruct((t, SLOTS), I32),
                   jax.ShapeDtypeStruct((t, SLOTS), F32)),
        compiler_params=_params("parallel"),
        name="peer_topk",
    )(qp, sk)


def _pack_kernel(w_ref, o_ref):
    n = o_ref.shape[0]
    even = w_ref[pl.ds(0, n, stride=2), :].astype(BF16).astype(F32)
    odd = w_ref[pl.ds(1, n, stride=2), :].astype(BF16).astype(F32)
    lo = (lax.bitcast_convert_type(even, I32) >> 16) & 0xFFFF
    o_ref[...] = lo | (lax.bitcast_convert_type(odd, I32) & HI_MASK)


def pack_table(w):
    n, d = w.shape
    rows = 2048
    packed = pl.pallas_call(
        _pack_kernel,
        grid=(n // (2 * rows), d // LANES),
        in_specs=[pl.BlockSpec((2 * rows, LANES), lambda i, c: (i, c))],
        out_specs=pl.BlockSpec((rows, LANES), lambda i, c: (i, c)),
        out_shape=jax.ShapeDtypeStruct((n // 2, d), I32),
        compiler_params=_params("parallel", "parallel"),
        name="pack_table",
    )(w)
    return packed.reshape(n // 2 * (d // LANES), LANES)


def _expert_row(tab_ref, row, shift):
    word = tab_ref[pl.ds(pl.multiple_of(row, SUBLANES), SUBLANES), :]
    return lax.bitcast_convert_type((word << shift) & HI_MASK, F32)


UNIT = SUBLANES


def _butterfly(parts, add, roll, select):
    step = SUBLANES // 2
    while len(parts) > 1:
        parts = [select(step, add(a, roll(a, step)), add(b, roll(b, SUBLANES - step)))
                 for a, b in zip(parts[0::2], parts[1::2])]
        step //= 2
    return parts[0]


def _butterfly_order():
    tiles = [[{p}] * SUBLANES for p in range(SUBLANES)]
    out = _butterfly(
        tiles,
        add=lambda a, b: [x | y for x, y in zip(a, b)],
        roll=lambda a, k: [a[(i - k) % SUBLANES] for i in range(SUBLANES)],
        select=lambda step, a, b: [a[i] if (i // step) % 2 else b[i] for i in range(SUBLANES)])
    order = [0] * SUBLANES
    for sublane, owners in enumerate(out):
        (p,) = owners
        order[p] = sublane
    return order


BUTTERFLY_ORDER = _butterfly_order()


def _rows_to_sublanes(parts):
    sub = lax.broadcasted_iota(I32, (SUBLANES, LANES), 0)
    placed = [None] * SUBLANES
    for i, part in enumerate(parts):
        placed[BUTTERFLY_ORDER.index(i)] = part
    return _butterfly(placed, add=lambda a, b: a + b, roll=lambda a, k: pltpu.roll(a, k, 0),
                      select=lambda step, a, b: jnp.where((sub // step) % 2 == 1, a, b))


def _peer_u_kernel(row_ref, shift_ref, m_ref, gate_ref, shiftv_ref, tab_ref, wcode_ref, part_ref, h_ref):
    tb = m_ref.shape[0]
    for t in range(tb):
        x = m_ref[t]
        for u in range(SLOTS // UNIT):
            parts = [_expert_row(tab_ref, row_ref[t, u * UNIT + e], shift_ref[t, u * UNIT + e]) * x
                     for e in range(UNIT)]
            part_ref[t, u * UNIT:(u + 1) * UNIT, :] = _rows_to_sublanes(parts)
    for t in range(tb):
        h_ref[t:t + 1, :] = jnp.sum(part_ref[t].T, axis=0, keepdims=True)
    h = h_ref[...]
    act = 0.5 * h * (1.0 + lax.erf(h * (2.0 ** -0.5)))
    w = (gate_ref[...] * act).astype(BF16).astype(F32)
    wcode_ref[...] = lax.bitcast_convert_type(w, I32) | shiftv_ref[...]


def peer_u(row, shift, m3, gate, tab):
    t = m3.shape[0]
    tb = min(EXPERT_TILE, t)
    smem = pl.BlockSpec((tb, SLOTS), lambda i: (i, 0), memory_space=pltpu.SMEM)
    vmem = pl.BlockSpec((tb, SLOTS), lambda i: (i, 0))
    return pl.pallas_call(
        _peer_u_kernel,
        grid=(t // tb,),
        in_specs=[smem, smem, pl.BlockSpec((tb, SUBLANES, LANES), lambda i: (i, 0, 0)), vmem, vmem,
                  pl.BlockSpec(tab.shape, lambda i: (0, 0), pipeline_mode=pl.Buffered(1))],
        out_specs=vmem,
        out_shape=jax.ShapeDtypeStruct((t, SLOTS), I32),
        scratch_shapes=[pltpu.VMEM((tb, SLOTS, LANES), F32), pltpu.VMEM((tb, SLOTS), F32)],
        compiler_params=_params("arbitrary"),
        name="peer_u",
    )(row, shift, m3, gate, shift, tab)


N_ACC = 4


def _peer_v_kernel(row_ref, wcode_ref, tab_ref, o_ref):
    tb = o_ref.shape[0]
    for t in range(tb):
        acc = [None] * N_ACC
        for j in range(SLOTS):
            code = jnp.full((SUBLANES, LANES), wcode_ref[t, j], I32)
            weight = lax.bitcast_convert_type(code & HI_MASK, F32)
            term = weight * _expert_row(tab_ref, row_ref[t, j], code & LOW_HALF_SHIFT)
            acc[j % N_ACC] = term if acc[j % N_ACC] is None else acc[j % N_ACC] + term
        o_ref[t] = (acc[0] + acc[1]) + (acc[2] + acc[3])


def peer_v(row, wcode, tab):
    t = row.shape[0]
    tb = min(EXPERT_TILE, t)
    smem = pl.BlockSpec((tb, SLOTS), lambda i: (i, 0), memory_space=pltpu.SMEM)
    return pl.pallas_call(
        _peer_v_kernel,
        grid=(t // tb,),
        in_specs=[smem, smem, pl.BlockSpec(tab.shape, lambda i: (0, 0), pipeline_mode=pl.Buffered(1))],
        out_specs=pl.BlockSpec((tb, SUBLANES, LANES), lambda i: (i, 0, 0)),
        out_shape=jax.ShapeDtypeStruct((t, SUBLANES, LANES), F32),
        compiler_params=_params("arbitrary"),
        name="peer_v",
    )(row, wcode, tab)


def _ple_kernel(h_ref, peer_ref, p_ref, wp_ref, wg_ref, gp_ref, gf_ref, y_ref):
    h = h_ref[...] + peer_ref[...]
    gate = jax.nn.sigmoid(_dot(_rms(h, gp_ref[...]).astype(BF16), wg_ref[...]))
    h = h + _dot(p_ref[...].astype(BF16), wp_ref[...]) * gate
    y_ref[...] = _rms(h, gf_ref[...])


def ple(h, peer_out, p, wp, wg, gp, gf):
    t = h.shape[0]
    tm = min(ROW_TILE, t)
    row = lambda n: pl.BlockSpec((tm, n), lambda i: (i, 0))
    return pl.pallas_call(
        _ple_kernel,
        grid=(t // tm,),
        in_specs=[row(D_MODEL), row(D_MODEL), row(PLE_DIM), _const_spec(wp.shape), _const_spec(wg.shape),
                  _const_spec((1, D_MODEL)), _const_spec((1, D_MODEL))],
        out_specs=row(D_MODEL),
        out_shape=jax.ShapeDtypeStruct((t, D_MODEL), F32),
        compiler_params=_params("parallel"),
        name="ple",
    )(h, peer_out, p, wp, wg, gp, gf)


def _t5_bucket(rel):
    n = jnp.maximum(rel, 0)
    max_exact = N_BUCKETS // 2
    nf = jnp.maximum(n, 1).astype(F32)
    large = max_exact + (jnp.log(nf / max_exact) / math.log(REL_MAX_DIST / max_exact)
                         * (N_BUCKETS - max_exact)).astype(I32)
    large = jnp.minimum(large, N_BUCKETS - 1)
    return jnp.where(n < max_exact, n, large)


def _band_bias(rel_bias, rel):
    far = rel_bias[N_BUCKETS - 1]
    bucket = _t5_bucket(rel)[..., None]
    b = jnp.zeros(rel.shape + (H_DIFF,), F32)
    for n in range(N_BUCKETS - 1):
        b = jnp.where(bucket == n, rel_bias[n] - far, b)
    b = jnp.where((rel >= 0)[..., None], b, NEG_INF)
    return jnp.moveaxis(b, -1, 0).astype(F32)


def _pad_rows(a, rows):
    return jnp.pad(a, ((0, 0), (0, rows - a.shape[1]), (0, 0)))


def _layer(x, p_l, conv_buf, ssm0, lam, w, attention):
    b, l, _ = x.shape
    t = b * l
    x2 = x.reshape(t, D_MODEL)
    z, xbc, q2, k, v, kb, vb, dt = in_proj(x2, w["g_mix"], w["wz"], w["wx"], w["wq"], w["wk"], w["wv"], w["wdt"])

    lp = -(-l // SSD_CHUNK) * SSD_CHUNK
    xbc3 = xbc.reshape(b, l, CONV_DIM)
    cbuf = jnp.pad(conv_buf, ((0, 0), (SUBLANES - (CONV_W - 1), 0), (0, 0)))
    y_ssm, ssm_final = ssd(_pad_rows(xbc3, lp), _pad_rows(z.reshape(b, l, D_SSM), lp),
                           _pad_rows(dt.reshape(b, l, LANES), lp), cbuf, ssm0,
                           w["conv_w"], w["conv_b"], w["dt_bias"], w["a_log"], w["d_skip"], w["g_ssm_norm"], l)
    y_ssm = y_ssm[:, :l].reshape(t, D_SSM)
    conv_new = jnp.concatenate([conv_buf, xbc3], axis=1)[:, -(CONV_W - 1):]

    y_diff = attention(q2, kb, vb)

    h1, m, qp = mid(x2, y_ssm, y_diff, w["wo_a"], w["wo_b"], w["g_ffn"], w["w_query"])
    row, shift, gate = peer_topk(qp, w["sub_keys"])
    wcode = peer_u(row, shift, m.reshape(t, SUBLANES, LANES), gate, w["tab_u"])
    peer_out = peer_v(row, wcode, w["tab_v"]).reshape(t, D_MODEL)
    y = ple(h1, peer_out, p_l.reshape(t, PLE_DIM), w["w_ple"], w["w_ple_gate"], w["g_ple"], w["g_final"])
    return (y.reshape(b, l, D_MODEL), k.reshape(b, l, H_DIFF, 2, DH), v.reshape(b, l, H_DIFF, 2 * DH),
            ssm_final, conv_new)


def kernel(x_prompt, x_sample, cache_k, cache_v, state_ssm, state_conv, page_table, p_prompt, p_sample, g_mix, w_in, conv_w, conv_b, dt_bias, a_log, d_skip, g_ssm_norm, lam_q1, lam_k1, lam_q2, lam_k2, g_diff_norm, rel_bias, w_out, g_ffn, w_query, sub_keys, expert_u, expert_v, w_ple, w_ple_gate, g_ple, g_final):
    assert w_in.shape[0] == 1, "single-layer problem"
    bsz, seq, _ = x_prompt.shape
    dec_b, dec_l, _ = x_sample.shape
    n_pages = page_table.shape[1]
    page = cache_k.shape[2]
    past = n_pages * page

    wi = w_in[0].astype(BF16)
    o_x = D_SSM
    o_dt = o_x + CONV_DIM
    o_q = o_dt + H_SSM
    o_k = o_q + D_DIFF
    o_v = o_k + D_DIFF
    lanes_of = lambda a: jnp.pad(a.astype(F32)[None, :], ((0, 0), (0, LANES - a.shape[0])))
    w = dict(
        g_mix=g_mix[0][None], wz=wi[:, :o_x], wx=wi[:, o_x:o_dt],
        wdt=jnp.pad(wi[:, o_dt:o_q], ((0, 0), (0, LANES - H_SSM))),
        wq=wi[:, o_q:o_k], wk=wi[:, o_k:o_v], wv=wi[:, o_v:],
        conv_w=conv_w[0], conv_b=conv_b[0][None], dt_bias=lanes_of(dt_bias[0]), a_log=lanes_of(a_log[0]),
        d_skip=lanes_of(d_skip[0]), g_ssm_norm=g_ssm_norm[0][None],
        wo_a=w_out[0][:D_SSM].astype(BF16), wo_b=w_out[0][D_SSM:].astype(BF16), g_ffn=g_ffn[0][None],
        w_query=w_query[0].astype(BF16), sub_keys=sub_keys[0].astype(BF16),
        tab_u=pack_table(expert_u[0]), tab_v=pack_table(expert_v[0]),
        w_ple=w_ple[0].astype(BF16), w_ple_gate=w_ple_gate[0].astype(BF16), g_ple=g_ple[0][None],
        g_final=g_final[None],
    )
    gd = g_diff_norm[0][None]
    lam = (jnp.exp(jnp.sum(lam_q1[0] * lam_k1[0])) - jnp.exp(jnp.sum(lam_q2[0] * lam_k2[0])) + LAM_INIT).reshape(1)

    tq = min(ATTN_TILE, seq)
    ii = jnp.arange(tq, dtype=I32)
    assert tq >= REL_MAX_DIST and page >= REL_MAX_DIST, "bias is only applied next to the diagonal"
    rel = jnp.stack([ii[None, :] - ii[:, None], tq + ii[None, :] - ii[:, None]])
    band_p = _band_bias(rel_bias, rel)

    def attn_prompt(q2, kb, vb):
        return prompt_attention(lam, q2, kb, vb, band_p, gd, bsz, seq, tq).reshape(bsz * seq, D_DIFF)

    yp, kp, vp, sp, cp = _layer(x_prompt, p_prompt[0], jnp.zeros((bsz, CONV_W - 1, CONV_DIM), F32),
                                jnp.zeros((bsz, H_SSM, P_SSM, N_STATE), F32), lam, w, attn_prompt)

    rows = dec_l * DEC_ROWS
    qpos = past + jnp.arange(rows, dtype=I32) // DEC_ROWS
    head_of = (jnp.arange(rows, dtype=I32) % DEC_ROWS) // 2
    kk = jnp.arange(page, dtype=I32)
    rel_last = qpos[:, None] - (past - page + kk)[None, :]
    rel_new = jnp.where(kk[None, :] < dec_l, qpos[:, None] - (past + kk)[None, :], -1)
    band_all = _band_bias(rel_bias, jnp.stack([rel_last, rel_new]))
    band_s = jnp.take_along_axis(band_all, head_of[None, None, :, None], axis=0)[0]
    lane_slot = jnp.arange(D_DIFF, dtype=I32) // DH
    row_slot = jnp.arange(rows, dtype=I32) % DEC_ROWS
    q_mask = (lane_slot[None, :] == row_slot[:, None])

    def attn_sample(q2, kb, vb):
        q = (q2[0] + q2[1]).reshape(dec_b, dec_l, D_DIFF)
        qst = jnp.where(q_mask[None], jnp.repeat(q, DEC_ROWS, axis=1), jnp.zeros((), BF16))
        kn = _pad_rows(kb.reshape(dec_b, dec_l, D_DIFF), page)
        vn = _pad_rows(vb.reshape(dec_b, dec_l, D_DIFF), page)
        o = sample_attention(page_table, lam, qst, jnp.transpose(cache_k[0], (0, 2, 3, 4, 1)), cache_v[0],
                             kn, vn, band_s, gd, dec_l)
        return o.reshape(dec_b * dec_l, D_DIFF)

    ys, ks, vs, ss, cs = _layer(x_sample, p_sample[0], state_conv[0], state_ssm[0], lam, w, attn_sample)

    return (yp, ys, kp[None], vp[None], sp[None], cp[None], ks[None], vs[None], ss[None], cs[None])
```

```python
import functools
import math

import jax
import jax.numpy as jnp
from jax import lax
from jax.experimental import pallas as pl
from jax.experimental.pallas import tpu as pltpu

F32 = jnp.float32
BF16 = jnp.bfloat16
I32 = jnp.int32

D_MODEL = 1024
D_SSM = 512
D_DIFF = 512
P_SSM = 64
H_SSM = 8
N_STATE = 128
SSM_GROUPS = 2
HEADS_PER_GROUP = H_SSM // SSM_GROUPS
CONV_W = 4
CONV_DIM = D_SSM + 2 * SSM_GROUPS * N_STATE
DH = 64
H_DIFF = 4
N_BUCKETS = 32
REL_MAX_DIST = 128
N_KEYS = 128
PEER_HEADS = 8
PEER_TOPK = 16
D_KEY = 256
PLE_DIM = 256
NEG_INF = -1e30
RMS_EPS = 1e-6
LAM_INIT = 0.8 - 0.6 * math.exp(-0.3 * 0)

LANES = 128
SUBLANES = 8
VMEM_LIMIT = 56 * 1024 * 1024

SSD_CHUNK = 128
ROW_TILE = 256
ATTN_TILE = 1024
PEER_TILE = 128
EXPERT_TILE = 16
SLOTS = PEER_HEADS * PEER_TOPK


def _params(*sem):
    return pltpu.CompilerParams(dimension_semantics=sem, vmem_limit_bytes=VMEM_LIMIT)


def _const_spec(shape):
    nd = len(shape)
    return pl.BlockSpec(shape, lambda *_: (0,) * nd)


def _rms(x, g):
    return x * lax.rsqrt(jnp.mean(x * x, axis=-1, keepdims=True) + RMS_EPS) * g


def _dot(a, b):
    return jnp.dot(a, b, preferred_element_type=F32)


def _dot_nt(a, b):
    return lax.dot_general(a, b, (((1,), (1,)), ((), ())), preferred_element_type=F32)


def _dot_tn(a, b):
    return lax.dot_general(a, b, (((0,), (0,)), ((), ())), preferred_element_type=F32)


def _in_proj_kernel(x_ref, g_ref, wz_ref, wx_ref, wq_ref, wk_ref, wv_ref, wdt_ref,
                    z_ref, xbc_ref, q2_ref, k_ref, v_ref, kb_ref, vb_ref, dt_ref):
    a = _rms(x_ref[...], g_ref[...]).astype(BF16)
    z_ref[...] = _dot(a, wz_ref[...])
    xbc_ref[...] = _dot(a, wx_ref[...])
    dt_ref[...] = _dot(a, wdt_ref[...])
    k = _dot(a, wk_ref[...])
    v = _dot(a, wv_ref[...])
    k_ref[...] = k
    v_ref[...] = v
    kb_ref[...] = k.astype(BF16)
    vb_ref[...] = v.astype(BF16)
    q = _dot(a, wq_ref[...]).astype(BF16) * (DH ** -0.5)
    lane = lax.broadcasted_iota(I32, q.shape, 1)
    first_map = (lane % (2 * DH)) < DH
    zero = jnp.zeros_like(q)
    q2_ref[0] = jnp.where(first_map, q, zero)
    q2_ref[1] = jnp.where(first_map, zero, q)


def in_proj(x, g, wz, wx, wq, wk, wv, wdt):
    t = x.shape[0]
    tm = min(ROW_TILE, t)
    row = lambda n: pl.BlockSpec((tm, n), lambda i: (i, 0))
    out_shape = (
        jax.ShapeDtypeStruct((t, D_SSM), F32),
        jax.ShapeDtypeStruct((t, CONV_DIM), F32),
        jax.ShapeDtypeStruct((2, t, D_DIFF), BF16),
        jax.ShapeDtypeStruct((t, D_DIFF), F32),
        jax.ShapeDtypeStruct((t, D_DIFF), F32),
        jax.ShapeDtypeStruct((t, D_DIFF), BF16),
        jax.ShapeDtypeStruct((t, D_DIFF), BF16),
        jax.ShapeDtypeStruct((t, LANES), F32),
    )
    return pl.pallas_call(
        _in_proj_kernel,
        grid=(t // tm,),
        in_specs=[row(D_MODEL), _const_spec((1, D_MODEL)), _const_spec(wz.shape), _const_spec(wx.shape),
                  _const_spec(wq.shape), _const_spec(wk.shape), _const_spec(wv.shape), _const_spec(wdt.shape)],
        out_specs=(row(D_SSM), row(CONV_DIM), pl.BlockSpec((2, tm, D_DIFF), lambda i: (0, i, 0)),
                   row(D_DIFF), row(D_DIFF), row(D_DIFF), row(D_DIFF), row(LANES)),
        out_shape=out_shape,
        compiler_params=_params("parallel"),
        name="in_proj",
    )(x, g, wz, wx, wq, wk, wv, wdt)


def _ssd_kernel(xbc_ref, z_ref, dt_ref, cbuf_ref, s0_ref, cw_ref, cb_ref, dtb_ref, alog_ref, dskip_ref, gn_ref,
                y_ref, sfin_ref, win_ref, state_ref, *, seq_len):
    c = pl.program_id(1)
    lc = SSD_CHUNK

    @pl.when(c == 0)
    def _():
        win_ref[0:SUBLANES, :] = cbuf_ref[0]
        state_ref[...] = s0_ref[0]

    win_ref[SUBLANES:SUBLANES + lc, :] = xbc_ref[0]
    base = SUBLANES - (CONV_W - 1)
    conv = cb_ref[...]
    for j in range(CONV_W):
        conv = conv + cw_ref[j:j + 1, :] * win_ref[base + j:base + j + lc, :]
    win_ref[0:SUBLANES, :] = win_ref[lc:lc + SUBLANES, :]
    xc = conv * jax.nn.sigmoid(conv)

    pos = c * lc + lax.broadcasted_iota(I32, (lc, LANES), 0)
    dt = jax.nn.softplus(dt_ref[0] + dtb_ref[...])
    dt = jnp.where(pos < seq_len, dt, 0.0)
    adt = -jnp.exp(alog_ref[...]) * dt
    row = lax.broadcasted_iota(I32, (lc, lc), 0)
    col = lax.broadcasted_iota(I32, (lc, lc), 1)
    causal = row >= col
    tri = causal.astype(F32)
    cs = jnp.dot(tri, adt, preferred_element_type=F32, precision=lax.Precision.HIGHEST)
    cs_t = cs.T
    cs_last = cs[lc - 1:lc, :]

    ys = []
    for g in range(SSM_GROUPS):
        b0 = D_SSM + g * N_STATE
        c0 = D_SSM + SSM_GROUPS * N_STATE + g * N_STATE
        bg = xc[:, b0:b0 + N_STATE]
        cg = xc[:, c0:c0 + N_STATE].astype(BF16)
        cb = _dot_nt(cg, bg.astype(BF16))
        for hh in range(HEADS_PER_GROUP):
            h = g * HEADS_PER_GROUP + hh
            xs = xc[:, h * P_SSM:(h + 1) * P_SSM]
            cs_h = cs[:, h:h + 1]
            decay = jnp.where(causal, jnp.exp(jnp.where(causal, cs_h - cs_t[h:h + 1, :], 0.0)), 0.0)
            xdt = xs * dt[:, h:h + 1]
            y = _dot((cb * decay).astype(BF16), xdt.astype(BF16))
            st = state_ref[h]
            y = y + jnp.exp(cs_h) * _dot_nt(cg, st.astype(BF16))
            y = y + dskip_ref[:, h:h + 1] * xs
            ys.append(y)
            to_end = jnp.exp(cs_last[:, h:h + 1] - cs_h)
            new = _dot_tn((xdt * to_end).astype(BF16), bg.astype(BF16))
            state_ref[h] = jnp.exp(cs_last[:, h:h + 1]) * st + new
    y = jnp.concatenate(ys, axis=1)
    zz = z_ref[0]
    y = y * (zz * jax.nn.sigmoid(zz))
    gw = D_SSM // SSM_GROUPS
    outs = []
    for g in range(SSM_GROUPS):
        yg = y[:, g * gw:(g + 1) * gw]
        outs.append(yg * lax.rsqrt(jnp.mean(yg * yg, axis=-1, keepdims=True) + RMS_EPS))
    y_ref[0] = (jnp.concatenate(outs, axis=1) * gn_ref[...]).astype(BF16)

    @pl.when(c == pl.num_programs(1) - 1)
    def _():
        sfin_ref[0] = state_ref[...]


def ssd(xbc, z, dt, cbuf, s0, cw, cb, dtb, alog, dskip, gn, seq_len):
    b, lp, _ = xbc.shape
    lc = SSD_CHUNK
    seq = lambda n: pl.BlockSpec((1, lc, n), lambda i, c: (i, c, 0))
    return pl.pallas_call(
        functools.partial(_ssd_kernel, seq_len=seq_len),
        grid=(b, lp // lc),
        in_specs=[seq(CONV_DIM), seq(D_SSM), seq(LANES),
                  pl.BlockSpec((1, SUBLANES, CONV_DIM), lambda i, c: (i, 0, 0)),
                  pl.BlockSpec((1, H_SSM, P_SSM, N_STATE), lambda i, c: (i, 0, 0, 0)),
                  _const_spec(cw.shape), _const_spec(cb.shape), _const_spec(dtb.shape), _const_spec(alog.shape),
                  _const_spec(dskip.shape), _const_spec(gn.shape)],
        out_specs=(seq(D_SSM), pl.BlockSpec((1, H_SSM, P_SSM, N_STATE), lambda i, c: (i, 0, 0, 0))),
        out_shape=(jax.ShapeDtypeStruct((b, lp, D_SSM), BF16),
                   jax.ShapeDtypeStruct((b, H_SSM, P_SSM, N_STATE), F32)),
        scratch_shapes=[pltpu.VMEM((SUBLANES + lc, CONV_DIM), F32), pltpu.VMEM((H_SSM, P_SSM, N_STATE), F32)],
        compiler_params=_params("parallel", "arbitrary"),
        name="ssd",
    )(xbc, z, dt, cbuf, s0, cw, cb, dtb, alog, dskip, gn)


def _softmax_tile(q, k, v, bias, m_ref, l_ref, acc_ref):
    s = _dot_nt(q, k)
    if bias is not None:
        s = s + bias
    m_old = m_ref[...]
    m_new = jnp.maximum(m_old, jnp.max(s, axis=-1, keepdims=True))
    alpha = jnp.exp(m_old - m_new)
    p = jnp.exp(s - m_new)
    l_ref[...] = alpha * l_ref[...] + jnp.sum(p, axis=-1, keepdims=True)
    acc_ref[...] = alpha * acc_ref[...] + _dot(p.astype(BF16), v)
    m_ref[...] = m_new


def _softmax_init(m_ref, l_ref, acc_ref):
    m_ref[...] = jnp.full(m_ref.shape, NEG_INF, F32)
    l_ref[...] = jnp.zeros(l_ref.shape, F32)
    acc_ref[...] = jnp.zeros(acc_ref.shape, F32)


def _attn_kernel(lam_ref, qt_ref, k_ref, vt_ref, band_ref, gd_ref, o_ref, m_ref, l_ref, acc_ref, *, tq):
    i = pl.program_id(2)
    qt = qt_ref[0, 0, 0]
    _softmax_init(m_ref, l_ref, acc_ref)

    def tile(j, bias):
        s = _dot(k_ref[0, j], qt)
        if bias is not None:
            s = s + jnp.concatenate([bias, bias], axis=1)
        m_old = m_ref[...]
        m_new = jnp.maximum(m_old, jnp.max(s, axis=0, keepdims=True))
        alpha = jnp.exp(m_old - m_new)
        p = jnp.exp(s - m_new)
        l_ref[...] = alpha * l_ref[...] + jnp.sum(p, axis=0, keepdims=True)
        acc_ref[...] = alpha * acc_ref[...] + _dot(vt_ref[0, 0, j], p.astype(BF16))
        m_ref[...] = m_new

    def far(j, carry):
        tile(j, None)
        return carry

    lax.fori_loop(0, jnp.maximum(i - 1, 0), far, 0)

    @pl.when(i >= 1)
    def _():
        tile(i - 1, band_ref[0, 1])

    tile(i, band_ref[0, 0])

    o = acc_ref[...] / l_ref[...]
    o = o[:, :tq] - lam_ref[0] * o[:, tq:]
    o = o * lax.rsqrt(jnp.mean(o * o, axis=0, keepdims=True) + RMS_EPS) * gd_ref[...] * (1.0 - LAM_INIT)
    o_ref[0] = o.T.astype(BF16)


def prompt_attention(lam, q2, kb, vb, band_t, gd, batch, seq, tq):
    nq = seq // tq
    head = 2 * DH
    qt = jnp.transpose(q2.reshape(2, batch, nq, tq, H_DIFF, head), (1, 4, 2, 5, 0, 3)).reshape(
        batch, H_DIFF, nq, head, 2 * tq)
    k4 = kb.reshape(batch, nq, tq, D_DIFF)
    vt = jnp.transpose(vb.reshape(batch, nq, tq, H_DIFF, head), (0, 3, 1, 4, 2))
    return pl.pallas_call(
        functools.partial(_attn_kernel, tq=tq),
        grid=(batch, H_DIFF, nq),
        in_specs=[pl.BlockSpec(memory_space=pltpu.SMEM),
                  pl.BlockSpec((1, 1, 1, head, 2 * tq), lambda b, h, i: (b, h, i, 0, 0)),
                  pl.BlockSpec((1, nq, tq, head), lambda b, h, i: (b, 0, 0, h)),
                  pl.BlockSpec((1, 1, nq, head, tq), lambda b, h, i: (b, h, 0, 0, 0)),
                  pl.BlockSpec((1, 2, tq, tq), lambda b, h, i: (h, 0, 0, 0)),
                  _const_spec((head, 1))],
        out_specs=pl.BlockSpec((1, tq, head), lambda b, h, i: (b, i, h)),
        out_shape=jax.ShapeDtypeStruct((batch, seq, D_DIFF), BF16),
        scratch_shapes=[pltpu.VMEM((1, 2 * tq), F32), pltpu.VMEM((1, 2 * tq), F32), pltpu.VMEM((head, 2 * tq), F32)],
        compiler_params=_params("parallel", "parallel", "arbitrary"),
        name="prompt_attention",
    )(lam, qt, k4, vt, band_t, gd.reshape(head, 1))


DEC_ROWS = 2 * H_DIFF


def _dec_attn_kernel(pt_ref, lam_ref, q_ref, *refs, n_groups, n_q, group):
    k_refs, v_refs = refs[:group], refs[group:2 * group]
    kn_ref, vn_ref, band_ref, gd_ref, o_ref, m_ref, l_ref, acc_ref = refs[2 * group:]
    j = pl.program_id(1)
    q = q_ref[0]
    page = band_ref.shape[2]

    @pl.when(j == 0)
    def _():
        _softmax_init(m_ref, l_ref, acc_ref)

    def update(s, weighted_values):
        m_old = m_ref[...]
        m_new = jnp.maximum(m_old, jnp.max(s, axis=-1, keepdims=True))
        alpha = jnp.exp(m_old - m_new)
        p = jnp.exp(s - m_new)
        l_ref[...] = alpha * l_ref[...] + jnp.sum(p, axis=-1, keepdims=True)
        acc_ref[...] = alpha * acc_ref[...] + weighted_values(p.astype(BF16))
        m_ref[...] = m_new

    scores = [_dot(q, k_ref[0].reshape(D_DIFF, page).astype(BF16)) for k_ref in k_refs]
    scores[-1] = scores[-1] + jnp.where(j == n_groups - 1, band_ref[0], 0.0)

    def cached_values(p):
        heads = []
        for h in range(H_DIFF):
            o = None
            for u, v_ref in enumerate(v_refs):
                part = _dot(p[:, u * page:(u + 1) * page], v_ref[0, :, h, :].astype(BF16))
                o = part if o is None else o + part
            heads.append(o)
        return jnp.concatenate(heads, axis=1)

    update(jnp.concatenate(scores, axis=1), cached_values)

    @pl.when(j == n_groups - 1)
    def _():
        update(_dot_nt(q, kn_ref[0]) + band_ref[1], lambda p: _dot(p, vn_ref[0]))
        rows = n_q * DEC_ROWS
        r = lax.broadcasted_iota(I32, (rows, D_DIFF), 0)
        lane = lax.broadcasted_iota(I32, (rows, D_DIFF), 1)
        own = (lane // (2 * DH)) == ((r % DEC_ROWS) // 2)
        w = jnp.where(r % 2 == 0, 1.0, -lam_ref[0])
        o = jnp.where(own, acc_ref[...] / l_ref[...] * w, 0.0)
        o = jnp.sum(o.reshape(n_q, DEC_ROWS, D_DIFF), axis=1)
        outs = []
        for h in range(H_DIFF):
            oh = o[:, h * 2 * DH:(h + 1) * 2 * DH]
            outs.append(_rms(oh, gd_ref[...]) * (1.0 - LAM_INIT))
        o_ref[0] = jnp.concatenate(outs, axis=1).astype(BF16)


PAGE_GROUP = 8


def sample_attention(page_table, lam, qst, cache_kt, cache_v, kn, vn, band, gd, n_q):
    b, n_pages = page_table.shape
    page = cache_v.shape[1]
    rows = n_q * DEC_ROWS
    group = math.gcd(PAGE_GROUP, n_pages)
    n_groups = n_pages // group

    def page_of(u):
        return lambda i, j, pt: (pt[i, j * group + u],)

    k_specs = [pl.BlockSpec((1,) + cache_kt.shape[1:], lambda i, j, pt, f=page_of(u): f(i, j, pt) + (0, 0, 0, 0))
               for u in range(group)]
    v_specs = [pl.BlockSpec((1,) + cache_v.shape[1:], lambda i, j, pt, f=page_of(u): f(i, j, pt) + (0, 0, 0))
               for u in range(group)]
    new_spec = pl.BlockSpec((1, page, D_DIFF), lambda i, j, pt: (i, 0, 0))
    grid_spec = pltpu.PrefetchScalarGridSpec(
        num_scalar_prefetch=1,
        grid=(b, n_groups),
        in_specs=[pl.BlockSpec(memory_space=pltpu.SMEM),
                  pl.BlockSpec((1, rows, D_DIFF), lambda i, j, pt: (i, 0, 0))]
                 + k_specs + v_specs
                 + [new_spec, new_spec,
                    pl.BlockSpec((2, rows, page), lambda i, j, pt: (0, 0, 0)),
                    pl.BlockSpec((1, 2 * DH), lambda i, j, pt: (0, 0))],
        out_specs=pl.BlockSpec((1, n_q, D_DIFF), lambda i, j, pt: (i, 0, 0)),
        scratch_shapes=[pltpu.VMEM((rows, 1), F32), pltpu.VMEM((rows, 1), F32), pltpu.VMEM((rows, D_DIFF), F32)],
    )
    return pl.pallas_call(
        functools.partial(_dec_attn_kernel, n_groups=n_groups, n_q=n_q, group=group),
        grid_spec=grid_spec,
        out_shape=jax.ShapeDtypeStruct((b, n_q, D_DIFF), BF16),
        compiler_params=_params("parallel", "arbitrary"),
        name="sample_attention",
    )(page_table, lam, qst, *([cache_kt] * group), *([cache_v] * group), kn, vn, band, gd)


def _mid_kernel(x_ref, ys_ref, yd_ref, wa_ref, wb_ref, g_ref, wq_ref, h_ref, m_ref, qp_ref):
    h = x_ref[...] + _dot(ys_ref[...], wa_ref[...]) + _dot(yd_ref[...], wb_ref[...])
    h_ref[...] = h
    m = _rms(h, g_ref[...]).astype(BF16)
    mf = m.astype(F32)
    for c in range(D_MODEL // LANES):
        m_ref[:, c, :] = mf[:, c * LANES:(c + 1) * LANES]
    qp_ref[...] = _dot(m, wq_ref[...]).astype(BF16)


def mid(x, ys, yd, wa, wb, g, wq):
    t = x.shape[0]
    tm = min(ROW_TILE, t)
    row = lambda n: pl.BlockSpec((tm, n), lambda i: (i, 0))
    nq = wq.shape[1]
    return pl.pallas_call(
        _mid_kernel,
        grid=(t // tm,),
        in_specs=[row(D_MODEL), row(D_SSM), row(D_DIFF), _const_spec(wa.shape), _const_spec(wb.shape),
                  _const_spec((1, D_MODEL)), _const_spec(wq.shape)],
        out_specs=(row(D_MODEL), pl.BlockSpec((tm, D_MODEL // LANES, LANES), lambda i: (i, 0, 0)), row(nq)),
        out_shape=(jax.ShapeDtypeStruct((t, D_MODEL), F32), jax.ShapeDtypeStruct((t, D_MODEL // LANES, LANES), F32),
                   jax.ShapeDtypeStruct((t, nq), BF16)),
        compiler_params=_params("parallel"),
        name="mid",
    )(x, ys, yd, wa, wb, g, wq)


def _top16(vals):
    rows = lax.broadcasted_iota(I32, vals.shape, 0)
    big = jnp.int32(vals.shape[0])
    out_v, out_i = [], []
    cur = vals
    for _ in range(PEER_TOPK):
        mx = jnp.max(cur, axis=0, keepdims=True)
        at = jnp.min(jnp.where(cur == mx, rows, big), axis=0, keepdims=True)
        out_v.append(mx)
        out_i.append(at)
        cur = jnp.where(rows == at, -jnp.inf, cur)
    return jnp.concatenate(out_v, axis=0), jnp.concatenate(out_i, axis=0)


HI_MASK = -65536
LOW_HALF_SHIFT = 16


def _topk_kernel(qp_ref, sk_ref, row_ref, shift_ref, gate_ref):
    tb = qp_ref.shape[0]
    half = D_KEY // 2
    idx_rows, gate_rows = [], []
    for h in range(PEER_HEADS):
        s, ix = [], []
        for c in range(2):
            qh = qp_ref[:, (2 * h + c) * half:(2 * h + c + 1) * half]
            sc = _dot_nt(sk_ref[c], qh)
            v, i = _top16(sc)
            s.append(v)
            ix.append(i)
        pieces = [(a, PEER_TOPK if a == 0 else SUBLANES) for a in range(SUBLANES)]
        cand = jnp.concatenate([s[0][a:a + 1, :] + s[1][:nb] for a, nb in pieces] + [s[0][SUBLANES:] + s[1][0:1]],
                               axis=0)
        cidx = jnp.concatenate([ix[0][a:a + 1, :] * N_KEYS + ix[1][:nb] for a, nb in pieces]
                               + [ix[0][SUBLANES:] * N_KEYS + ix[1][0:1]], axis=0)
        sc, pos = _top16(cand)
        prow = lax.broadcasted_iota(I32, cand.shape, 0)
        idx = jnp.concatenate(
            [jnp.max(jnp.where(prow == pos[k:k + 1, :], cidx, -1), axis=0, keepdims=True) for k in range(PEER_TOPK)],
            axis=0)
        e = jnp.exp(sc - sc[0:1, :])
        gate_rows.append(e / jnp.sum(e, axis=0, keepdims=True))
        idx_rows.append(idx)
    idx = jnp.concatenate(idx_rows, axis=0).T
    gate_ref[...] = jnp.concatenate(gate_rows, axis=0).T
    row_ref[...] = (idx >> 1) * SUBLANES
    shift_ref[...] = ((idx & 1) ^ 1) * LOW_HALF_SHIFT


def peer_topk(qp, sk):
    t = qp.shape[0]
    tb = min(PEER_TILE, t)
    out = pl.BlockSpec((tb, SLOTS), lambda i: (i, 0))
    return pl.pallas_call(
        _topk_kernel,
        grid=(t // tb,),
        in_specs=[pl.BlockSpec((tb, qp.shape[1]), lambda i: (i, 0)), _const_spec(sk.shape)],
        out_specs=(out, out, out),
        out_shape=(jax.ShapeDtypeStruct((t, SLOTS), I32), jax.ShapeDtypeStruct((t, SLOTS), I32),
                   jax.ShapeDtypeStruct((t, SLOTS), F32)),
        compiler_params=_params("parallel"),
        name="peer_topk",
    )(qp, sk)


def _pack_kernel(w_ref, o_ref):
    n = o_ref.shape[0]
    even = w_ref[pl.ds(0, n, stride=2), :].astype(BF16).astype(F32)
    odd = w_ref[pl.ds(1, n, stride=2), :].astype(BF16).astype(F32)
    lo = (lax.bitcast_convert_type(even, I32) >> 16) & 0xFFFF
    o_ref[...] = lo | (lax.bitcast_convert_type(odd, I32) & HI_MASK)


def pack_table(w):
    n, d = w.shape
    rows = 2048
    packed = pl.pallas_call(
        _pack_kernel,
        grid=(n // (2 * rows), d // LANES),
        in_specs=[pl.BlockSpec((2 * rows, LANES), lambda i, c: (i, c))],
        out_specs=pl.BlockSpec((rows, LANES), lambda i, c: (i, c)),
        out_shape=jax.ShapeDtypeStruct((n // 2, d), I32),
        compiler_params=_params("parallel", "parallel"),
        name="pack_table",
    )(w)
    return packed.reshape(n // 2 * (d // LANES), LANES)


def _expert_row(tab_ref, row, shift):
    word = tab_ref[pl.ds(pl.multiple_of(row, SUBLANES), SUBLANES), :]
    return lax.bitcast_convert_type((word << shift) & HI_MASK, F32)


UNIT = SUBLANES


def _butterfly(parts, add, roll, select):
    step = SUBLANES // 2
    while len(parts) > 1:
        parts = [select(step, add(a, roll(a, step)), add(b, roll(b, SUBLANES - step)))
                 for a, b in zip(parts[0::2], parts[1::2])]
        step //= 2
    return parts[0]


def _butterfly_order():
    tiles = [[{p}] * SUBLANES for p in range(SUBLANES)]
    out = _butterfly(
        tiles,
        add=lambda a, b: [x | y for x, y in zip(a, b)],
        roll=lambda a, k: [a[(i - k) % SUBLANES] for i in range(SUBLANES)],
        select=lambda step, a, b: [a[i] if (i // step) % 2 else b[i] for i in range(SUBLANES)])
    order = [0] * SUBLANES
    for sublane, owners in enumerate(out):
        (p,) = owners
        order[p] = sublane
    return order


BUTTERFLY_ORDER = _butterfly_order()


def _rows_to_sublanes(parts):
    sub = lax.broadcasted_iota(I32, (SUBLANES, LANES), 0)
    placed = [None] * SUBLANES
    for i, part in enumerate(parts):
        placed[BUTTERFLY_ORDER.index(i)] = part
    return _butterfly(placed, add=lambda a, b: a + b, roll=lambda a, k: pltpu.roll(a, k, 0),
                      select=lambda step, a, b: jnp.where((sub // step) % 2 == 1, a, b))


def _peer_u_kernel(row_ref, shift_ref, m_ref, gate_ref, shiftv_ref, tab_ref, wt_ref, part_ref, h_ref, pad_ref):
    tb = m_ref.shape[0]

    @pl.when(pl.program_id(0) == 0)
    def _():
        pad_ref[...] = jnp.zeros(pad_ref.shape, F32)

    def token(t):
        x = m_ref[t]
        for u in range(SLOTS // UNIT):
            parts = [_expert_row(tab_ref, row_ref[t, u * UNIT + e], shift_ref[t, u * UNIT + e]) * x
                     for e in range(UNIT)]
            part_ref[t, u * UNIT:(u + 1) * UNIT, :] = _rows_to_sublanes(parts)

    for t in range(tb):
        token(t)
    for t in range(tb):
        h_ref[t:t + 1, :] = jnp.sum(part_ref[t].T, axis=0, keepdims=True)
    h = h_ref[...]
    act = 0.5 * h * (1.0 + lax.erf(h * (2.0 ** -0.5)))
    w = (gate_ref[...] * act).astype(BF16).astype(F32)
    low = shiftv_ref[...] == LOW_HALF_SHIFT
    pad_ref[0:tb, :] = jnp.where(low, w, 0.0)
    pad_ref[tb:2 * tb, :] = jnp.where(low, 0.0, w)
    wt_ref[0] = pad_ref[...].T


def peer_u(row, shift, m3, gate, tab):
    t = m3.shape[0]
    tb = min(EXPERT_TILE, t)
    smem = pl.BlockSpec((tb, SLOTS), lambda i: (i, 0), memory_space=pltpu.SMEM)
    vmem = pl.BlockSpec((tb, SLOTS), lambda i: (i, 0))
    return pl.pallas_call(
        _peer_u_kernel,
        grid=(t // tb,),
        in_specs=[smem, smem, pl.BlockSpec((tb, SUBLANES, LANES), lambda i: (i, 0, 0)), vmem, vmem,
                  pl.BlockSpec(tab.shape, lambda i: (0, 0), pipeline_mode=pl.Buffered(1))],
        out_specs=pl.BlockSpec((1, SLOTS, LANES), lambda i: (i, 0, 0)),
        out_shape=jax.ShapeDtypeStruct((t // tb, SLOTS, LANES), F32),
        scratch_shapes=[pltpu.VMEM((tb, SLOTS, LANES), F32), pltpu.VMEM((tb, SLOTS), F32),
                        pltpu.VMEM((LANES, SLOTS), F32)],
        compiler_params=_params("arbitrary"),
        name="peer_u",
    )(row, shift, m3, gate, shift, tab)


N_ACC = 4


REGION_TOKENS = 4


def _token_regions(n, body):
    always = pl.program_id(0) >= 0

    def region(first):
        for t in range(first, min(first + REGION_TOKENS, n)):
            body(t)

    for first in range(0, n, REGION_TOKENS):
        pl.when(always)(functools.partial(region, first))


def _peer_v_kernel(row_ref, wt_ref, spread_ref, tab_ref, o_ref, splat_ref):
    tb = o_ref.shape[0]
    splat_ref[...] = _dot(wt_ref[0][:, :2 * tb].astype(BF16), spread_ref[...])

    def token(t):
        acc = [None] * N_ACC
        for j in range(SLOTS):
            word = tab_ref[pl.ds(pl.multiple_of(row_ref[t, j], SUBLANES), SUBLANES), :]
            low = lax.bitcast_convert_type(word << LOW_HALF_SHIFT, F32)
            high = lax.bitcast_convert_type(word & HI_MASK, F32)
            w_low = jnp.broadcast_to(splat_ref[j:j + 1, t * LANES:(t + 1) * LANES], (SUBLANES, LANES))
            w_high = jnp.broadcast_to(splat_ref[j:j + 1, (tb + t) * LANES:(tb + t + 1) * LANES], (SUBLANES, LANES))
            term = w_low * low + w_high * high
            acc[j % N_ACC] = term if acc[j % N_ACC] is None else acc[j % N_ACC] + term
        o_ref[t] = (acc[0] + acc[1]) + (acc[2] + acc[3])

    _token_regions(tb, token)


def peer_v(row, wt, tab):
    t = row.shape[0]
    tb = min(EXPERT_TILE, t)
    smem = pl.BlockSpec((tb, SLOTS), lambda i: (i, 0), memory_space=pltpu.SMEM)
    spread = jnp.repeat(jnp.eye(2 * tb, dtype=BF16), LANES, axis=1)
    return pl.pallas_call(
        _peer_v_kernel,
        grid=(t // tb,),
        in_specs=[smem, pl.BlockSpec((1, SLOTS, LANES), lambda i: (i, 0, 0)), _const_spec(spread.shape),
                  pl.BlockSpec(tab.shape, lambda i: (0, 0), pipeline_mode=pl.Buffered(1))],
        out_specs=pl.BlockSpec((tb, SUBLANES, LANES), lambda i: (i, 0, 0)),
        out_shape=jax.ShapeDtypeStruct((t, SUBLANES, LANES), F32),
        scratch_shapes=[pltpu.VMEM((SLOTS, 2 * tb * LANES), F32)],
        compiler_params=_params("arbitrary"),
        name="peer_v",
    )(row, wt, spread, tab)


def _ple_kernel(h_ref, peer_ref, p_ref, wp_ref, wg_ref, gp_ref, gf_ref, y_ref):
    peer = jnp.concatenate([peer_ref[:, c, :] for c in range(D_MODEL // LANES)], axis=1)
    h = h_ref[...] + peer
    gate = jax.nn.sigmoid(_dot(_rms(h, gp_ref[...]).astype(BF16), wg_ref[...]))
    h = h + _dot(p_ref[...].astype(BF16), wp_ref[...]) * gate
    y_ref[...] = _rms(h, gf_ref[...])


def ple(h, peer_out, p, wp, wg, gp, gf):
    t = h.shape[0]
    tm = min(ROW_TILE, t)
    row = lambda n: pl.BlockSpec((tm, n), lambda i: (i, 0))
    return pl.pallas_call(
        _ple_kernel,
        grid=(t // tm,),
        in_specs=[row(D_MODEL), pl.BlockSpec((tm, D_MODEL // LANES, LANES), lambda i: (i, 0, 0)), row(PLE_DIM),
                  _const_spec(wp.shape), _const_spec(wg.shape),
                  _const_spec((1, D_MODEL)), _const_spec((1, D_MODEL))],
        out_specs=row(D_MODEL),
        out_shape=jax.ShapeDtypeStruct((t, D_MODEL), F32),
        compiler_params=_params("parallel"),
        name="ple",
    )(h, peer_out, p, wp, wg, gp, gf)


def _t5_bucket(rel):
    n = jnp.maximum(rel, 0)
    max_exact = N_BUCKETS // 2
    nf = jnp.maximum(n, 1).astype(F32)
    large = max_exact + (jnp.log(nf / max_exact) / math.log(REL_MAX_DIST / max_exact)
                         * (N_BUCKETS - max_exact)).astype(I32)
    large = jnp.minimum(large, N_BUCKETS - 1)
    return jnp.where(n < max_exact, n, large)


def _band_bias(rel_bias, rel):
    far = rel_bias[N_BUCKETS - 1]
    bucket = _t5_bucket(rel)[..., None]
    b = jnp.zeros(rel.shape + (H_DIFF,), F32)
    for n in range(N_BUCKETS - 1):
        b = jnp.where(bucket == n, rel_bias[n] - far, b)
    b = jnp.where((rel >= 0)[..., None], b, NEG_INF)
    return jnp.moveaxis(b, -1, 0).astype(F32)


def _pad_rows(a, rows):
    return jnp.pad(a, ((0, 0), (0, rows - a.shape[1]), (0, 0)))


def _layer(x, p_l, conv_buf, ssm0, lam, w, attention):
    b, l, _ = x.shape
    t = b * l
    x2 = x.reshape(t, D_MODEL)
    z, xbc, q2, k, v, kb, vb, dt = in_proj(x2, w["g_mix"], w["wz"], w["wx"], w["wq"], w["wk"], w["wv"], w["wdt"])

    lp = -(-l // SSD_CHUNK) * SSD_CHUNK
    xbc3 = xbc.reshape(b, l, CONV_DIM)
    cbuf = jnp.pad(conv_buf, ((0, 0), (SUBLANES - (CONV_W - 1), 0), (0, 0)))
    y_ssm, ssm_final = ssd(_pad_rows(xbc3, lp), _pad_rows(z.reshape(b, l, D_SSM), lp),
                           _pad_rows(dt.reshape(b, l, LANES), lp), cbuf, ssm0,
                           w["conv_w"], w["conv_b"], w["dt_bias"], w["a_log"], w["d_skip"], w["g_ssm_norm"], l)
    y_ssm = y_ssm[:, :l].reshape(t, D_SSM)
    tail = CONV_W - 1
    conv_new = xbc3[:, l - tail:] if l >= tail else jnp.concatenate([conv_buf, xbc3], axis=1)[:, -tail:]

    y_diff = attention(q2, kb, vb)

    h1, m, qp = mid(x2, y_ssm, y_diff, w["wo_a"], w["wo_b"], w["g_ffn"], w["w_query"])
    row, shift, gate = peer_topk(qp, w["sub_keys"])
    wt = peer_u(row, shift, m, gate, w["tab_u"])
    peer_out = peer_v(row, wt, w["tab_v"])
    y = ple(h1, peer_out, p_l.reshape(t, PLE_DIM), w["w_ple"], w["w_ple_gate"], w["g_ple"], w["g_final"])
    return (y.reshape(b, l, D_MODEL), k.reshape(b, l, H_DIFF, 2, DH), v.reshape(b, l, H_DIFF, 2 * DH),
            ssm_final, conv_new)


def kernel(x_prompt, x_sample, cache_k, cache_v, state_ssm, state_conv, page_table, p_prompt, p_sample, g_mix, w_in, conv_w, conv_b, dt_bias, a_log, d_skip, g_ssm_norm, lam_q1, lam_k1, lam_q2, lam_k2, g_diff_norm, rel_bias, w_out, g_ffn, w_query, sub_keys, expert_u, expert_v, w_ple, w_ple_gate, g_ple, g_final):
    assert w_in.shape[0] == 1, "single-layer problem"
    bsz, seq, _ = x_prompt.shape
    dec_b, dec_l, _ = x_sample.shape
    n_pages = page_table.shape[1]
    page = cache_k.shape[2]
    past = n_pages * page

    wi = w_in[0].astype(BF16)
    o_x = D_SSM
    o_dt = o_x + CONV_DIM
    o_q = o_dt + H_SSM
    o_k = o_q + D_DIFF
    o_v = o_k + D_DIFF
    lanes_of = lambda a: jnp.pad(a.astype(F32)[None, :], ((0, 0), (0, LANES - a.shape[0])))
    w = dict(
        g_mix=g_mix[0][None], wz=wi[:, :o_x], wx=wi[:, o_x:o_dt],
        wdt=jnp.pad(wi[:, o_dt:o_q], ((0, 0), (0, LANES - H_SSM))),
        wq=wi[:, o_q:o_k], wk=wi[:, o_k:o_v], wv=wi[:, o_v:],
        conv_w=conv_w[0], conv_b=conv_b[0][None], dt_bias=lanes_of(dt_bias[0]), a_log=lanes_of(a_log[0]),
        d_skip=lanes_of(d_skip[0]), g_ssm_norm=g_ssm_norm[0][None],
        wo_a=w_out[0][:D_SSM].astype(BF16), wo_b=w_out[0][D_SSM:].astype(BF16), g_ffn=g_ffn[0][None],
        w_query=w_query[0].astype(BF16), sub_keys=sub_keys[0].astype(BF16),
        tab_u=pack_table(expert_u[0]), tab_v=pack_table(expert_v[0]),
        w_ple=w_ple[0].astype(BF16), w_ple_gate=w_ple_gate[0].astype(BF16), g_ple=g_ple[0][None],
        g_final=g_final[None],
    )
    gd = g_diff_norm[0][None]
    lam = (jnp.exp(jnp.sum(lam_q1[0] * lam_k1[0])) - jnp.exp(jnp.sum(lam_q2[0] * lam_k2[0])) + LAM_INIT).reshape(1)

    tq = min(ATTN_TILE, seq)
    ii = jnp.arange(tq, dtype=I32)
    assert tq >= REL_MAX_DIST and page >= REL_MAX_DIST, "bias is only applied next to the diagonal"
    rel = jnp.stack([ii[None, :] - ii[:, None], tq + ii[None, :] - ii[:, None]])
    band_p = _band_bias(rel_bias, rel)

    def attn_prompt(q2, kb, vb):
        return prompt_attention(lam, q2, kb, vb, band_p, gd, bsz, seq, tq).reshape(bsz * seq, D_DIFF)

    yp, kp, vp, sp, cp = _layer(x_prompt, p_prompt[0], jnp.zeros((bsz, CONV_W - 1, CONV_DIM), F32),
                                jnp.zeros((bsz, H_SSM, P_SSM, N_STATE), F32), lam, w, attn_prompt)

    rows = dec_l * DEC_ROWS
    qpos = past + jnp.arange(rows, dtype=I32) // DEC_ROWS
    head_of = (jnp.arange(rows, dtype=I32) % DEC_ROWS) // 2
    kk = jnp.arange(page, dtype=I32)
    rel_last = qpos[:, None] - (past - page + kk)[None, :]
    rel_new = jnp.where(kk[None, :] < dec_l, qpos[:, None] - (past + kk)[None, :], -1)
    band_all = _band_bias(rel_bias, jnp.stack([rel_last, rel_new]))
    band_s = jnp.take_along_axis(band_all, head_of[None, None, :, None], axis=0)[0]
    lane_slot = jnp.arange(D_DIFF, dtype=I32) // DH
    row_slot = jnp.arange(rows, dtype=I32) % DEC_ROWS
    q_mask = (lane_slot[None, :] == row_slot[:, None])

    def attn_sample(q2, kb, vb):
        q = (q2[0] + q2[1]).reshape(dec_b, dec_l, D_DIFF)
        qst = jnp.where(q_mask[None], jnp.repeat(q, DEC_ROWS, axis=1), jnp.zeros((), BF16))
        kn = _pad_rows(kb.reshape(dec_b, dec_l, D_DIFF), page)
        vn = _pad_rows(vb.reshape(dec_b, dec_l, D_DIFF), page)
        o = sample_attention(page_table, lam, qst, jnp.transpose(cache_k[0], (0, 2, 3, 4, 1)), cache_v[0],
                             kn, vn, band_s, gd, dec_l)
        return o.reshape(dec_b * dec_l, D_DIFF)

    ys, ks, vs, ss, cs = _layer(x_sample, p_sample[0], state_conv[0], state_ssm[0], lam, w, attn_sample)

    return (yp, ys, kp[None], vp[None], sp[None], cp[None], ks[None], vs[None], ss[None], cs[None])
```

```python
import functools
import math

import jax
import jax.numpy as jnp
from jax import lax
from jax.experimental import pallas as pl
from jax.experimental.pallas import tpu as pltpu

F32 = jnp.float32
BF16 = jnp.bfloat16
I32 = jnp.int32

D_MODEL = 1024
D_SSM = 512
D_DIFF = 512
P_SSM = 64
H_SSM = 8
N_STATE = 128
SSM_GROUPS = 2
HEADS_PER_GROUP = H_SSM // SSM_GROUPS
CONV_W = 4
CONV_DIM = D_SSM + 2 * SSM_GROUPS * N_STATE
DH = 64
H_DIFF = 4
N_BUCKETS = 32
REL_MAX_DIST = 128
N_KEYS = 128
PEER_HEADS = 8
PEER_TOPK = 16
D_KEY = 256
PLE_DIM = 256
NEG_INF = -1e30
RMS_EPS = 1e-6
LAM_INIT = 0.8 - 0.6 * math.exp(-0.3 * 0)

LANES = 128
SUBLANES = 8
VMEM_LIMIT = 56 * 1024 * 1024

SSD_CHUNK = 128
ROW_TILE = 256
ATTN_TILE = 1024
PEER_TILE = 128
EXPERT_TILE = 16
SLOTS = PEER_HEADS * PEER_TOPK


def _params(*sem):
    return pltpu.CompilerParams(dimension_semantics=sem, vmem_limit_bytes=VMEM_LIMIT)


def _const_spec(shape):
    nd = len(shape)
    return pl.BlockSpec(shape, lambda *_: (0,) * nd)


def _rms(x, g):
    return x * lax.rsqrt(jnp.mean(x * x, axis=-1, keepdims=True) + RMS_EPS) * g


def _dot(a, b):
    return jnp.dot(a, b, preferred_element_type=F32)


def _dot_nt(a, b):
    return lax.dot_general(a, b, (((1,), (1,)), ((), ())), preferred_element_type=F32)


def _dot_tn(a, b):
    return lax.dot_general(a, b, (((0,), (0,)), ((), ())), preferred_element_type=F32)


def _in_proj_kernel(x_ref, g_ref, wz_ref, wx_ref, wq_ref, wk_ref, wv_ref, wdt_ref,
                    z_ref, xbc_ref, q2_ref, k_ref, v_ref, kb_ref, vb_ref, dt_ref, *, transposed_k):
    a = _rms(x_ref[...], g_ref[...]).astype(BF16)
    z_ref[...] = _dot(a, wz_ref[...])
    xbc_ref[...] = _dot(a, wx_ref[...])
    dt_ref[...] = _dot(a, wdt_ref[...])
    k = _dot(a, wk_ref[...])
    v = _dot(a, wv_ref[...])
    if transposed_k:
        k_ref[0] = k.T
    else:
        k_ref[...] = k
    for h in range(H_DIFF):
        v_ref[:, h, :] = v[:, h * 2 * DH:(h + 1) * 2 * DH]
    kb_ref[...] = k.astype(BF16)
    vb_ref[...] = v.astype(BF16)
    q = _dot(a, wq_ref[...]).astype(BF16) * (DH ** -0.5)
    lane = lax.broadcasted_iota(I32, q.shape, 1)
    first_map = (lane % (2 * DH)) < DH
    zero = jnp.zeros_like(q)
    q2_ref[0] = jnp.where(first_map, q, zero)
    q2_ref[1] = jnp.where(first_map, zero, q)


def in_proj(x, g, wz, wx, wq, wk, wv, wdt, seq):
    t = x.shape[0]
    tm = min(ROW_TILE, t)
    row = lambda n: pl.BlockSpec((tm, n), lambda i: (i, 0))
    transposed_k = seq % tm == 0
    per_seq = seq // tm if transposed_k else 1
    if transposed_k:
        k_shape = jax.ShapeDtypeStruct((t // seq, D_DIFF, seq), F32)
        k_spec = pl.BlockSpec((1, D_DIFF, tm), lambda i: (i // per_seq, 0, i % per_seq))
    else:
        k_shape, k_spec = jax.ShapeDtypeStruct((t, D_DIFF), F32), row(D_DIFF)
    out_shape = (
        jax.ShapeDtypeStruct((t, D_SSM), F32),
        jax.ShapeDtypeStruct((t, CONV_DIM), F32),
        jax.ShapeDtypeStruct((2, t, D_DIFF), BF16),
        k_shape,
        jax.ShapeDtypeStruct((t, H_DIFF, 2 * DH), F32),
        jax.ShapeDtypeStruct((t, D_DIFF), BF16),
        jax.ShapeDtypeStruct((t, D_DIFF), BF16),
        jax.ShapeDtypeStruct((t, LANES), F32),
    )
    return pl.pallas_call(
        functools.partial(_in_proj_kernel, transposed_k=transposed_k),
        grid=(t // tm,),
        in_specs=[row(D_MODEL), _const_spec((1, D_MODEL)), _const_spec(wz.shape), _const_spec(wx.shape),
                  _const_spec(wq.shape), _const_spec(wk.shape), _const_spec(wv.shape), _const_spec(wdt.shape)],
        out_specs=(row(D_SSM), row(CONV_DIM), pl.BlockSpec((2, tm, D_DIFF), lambda i: (0, i, 0)),
                   k_spec, pl.BlockSpec((tm, H_DIFF, 2 * DH), lambda i: (i, 0, 0)), row(D_DIFF), row(D_DIFF),
                   row(LANES)),
        out_shape=out_shape,
        compiler_params=_params("parallel"),
        name="in_proj",
    )(x, g, wz, wx, wq, wk, wv, wdt)


def _ssd_kernel(xbc_ref, z_ref, dt_ref, cbuf_ref, s0_ref, cw_ref, cb_ref, dtb_ref, alog_ref, dskip_ref, gn_ref,
                y_ref, sfin_ref, win_ref, state_ref, *, seq_len):
    c = pl.program_id(1)
    lc = SSD_CHUNK

    @pl.when(c == 0)
    def _():
        win_ref[0:SUBLANES, :] = cbuf_ref[0]
        state_ref[...] = s0_ref[0]

    win_ref[SUBLANES:SUBLANES + lc, :] = xbc_ref[0]
    base = SUBLANES - (CONV_W - 1)
    conv = cb_ref[...]
    for j in range(CONV_W):
        conv = conv + cw_ref[j:j + 1, :] * win_ref[base + j:base + j + lc, :]
    win_ref[0:SUBLANES, :] = win_ref[lc:lc + SUBLANES, :]
    xc = conv * jax.nn.sigmoid(conv)

    pos = c * lc + lax.broadcasted_iota(I32, (lc, LANES), 0)
    dt = jax.nn.softplus(dt_ref[0] + dtb_ref[...])
    dt = jnp.where(pos < seq_len, dt, 0.0)
    adt = -jnp.exp(alog_ref[...]) * dt
    row = lax.broadcasted_iota(I32, (lc, lc), 0)
    col = lax.broadcasted_iota(I32, (lc, lc), 1)
    causal = row >= col
    tri = causal.astype(F32)
    cs = jnp.dot(tri, adt, preferred_element_type=F32, precision=lax.Precision.HIGHEST)
    cs_t = cs.T
    cs_last = cs[lc - 1:lc, :]

    ys = []
    for g in range(SSM_GROUPS):
        b0 = D_SSM + g * N_STATE
        c0 = D_SSM + SSM_GROUPS * N_STATE + g * N_STATE
        bg = xc[:, b0:b0 + N_STATE]
        cg = xc[:, c0:c0 + N_STATE].astype(BF16)
        cb = _dot_nt(cg, bg.astype(BF16))
        for hh in range(HEADS_PER_GROUP):
            h = g * HEADS_PER_GROUP + hh
            xs = xc[:, h * P_SSM:(h + 1) * P_SSM]
            cs_h = cs[:, h:h + 1]
            decay = jnp.where(causal, jnp.exp(jnp.where(causal, cs_h - cs_t[h:h + 1, :], 0.0)), 0.0)
            xdt = xs * dt[:, h:h + 1]
            y = _dot((cb * decay).astype(BF16), xdt.astype(BF16))
            st = state_ref[h]
            y = y + jnp.exp(cs_h) * _dot_nt(cg, st.astype(BF16))
            y = y + dskip_ref[:, h:h + 1] * xs
            ys.append(y)
            to_end = jnp.exp(cs_last[:, h:h + 1] - cs_h)
            new = _dot_tn((xdt * to_end).astype(BF16), bg.astype(BF16))
            state_ref[h] = jnp.exp(cs_last[:, h:h + 1]) * st + new
    y = jnp.concatenate(ys, axis=1)
    zz = z_ref[0]
    y = y * (zz * jax.nn.sigmoid(zz))
    gw = D_SSM // SSM_GROUPS
    outs = []
    for g in range(SSM_GROUPS):
        yg = y[:, g * gw:(g + 1) * gw]
        outs.append(yg * lax.rsqrt(jnp.mean(yg * yg, axis=-1, keepdims=True) + RMS_EPS))
    y_ref[0] = (jnp.concatenate(outs, axis=1) * gn_ref[...]).astype(BF16)

    @pl.when(c == pl.num_programs(1) - 1)
    def _():
        sfin_ref[0] = state_ref[...]


def ssd(xbc, z, dt, cbuf, s0, cw, cb, dtb, alog, dskip, gn, seq_len):
    b, lp, _ = xbc.shape
    lc = SSD_CHUNK
    seq = lambda n: pl.BlockSpec((1, lc, n), lambda i, c: (i, c, 0))
    return pl.pallas_call(
        functools.partial(_ssd_kernel, seq_len=seq_len),
        grid=(b, lp // lc),
        in_specs=[seq(CONV_DIM), seq(D_SSM), seq(LANES),
                  pl.BlockSpec((1, SUBLANES, CONV_DIM), lambda i, c: (i, 0, 0)),
                  pl.BlockSpec((1, H_SSM, P_SSM, N_STATE), lambda i, c: (i, 0, 0, 0)),
                  _const_spec(cw.shape), _const_spec(cb.shape), _const_spec(dtb.shape), _const_spec(alog.shape),
                  _const_spec(dskip.shape), _const_spec(gn.shape)],
        out_specs=(seq(D_SSM), pl.BlockSpec((1, H_SSM, P_SSM, N_STATE), lambda i, c: (i, 0, 0, 0))),
        out_shape=(jax.ShapeDtypeStruct((b, lp, D_SSM), BF16),
                   jax.ShapeDtypeStruct((b, H_SSM, P_SSM, N_STATE), F32)),
        scratch_shapes=[pltpu.VMEM((SUBLANES + lc, CONV_DIM), F32), pltpu.VMEM((H_SSM, P_SSM, N_STATE), F32)],
        compiler_params=_params("parallel", "arbitrary"),
        name="ssd",
    )(xbc, z, dt, cbuf, s0, cw, cb, dtb, alog, dskip, gn)


def _softmax_tile(q, k, v, bias, m_ref, l_ref, acc_ref):
    s = _dot_nt(q, k)
    if bias is not None:
        s = s + bias
    m_old = m_ref[...]
    m_new = jnp.maximum(m_old, jnp.max(s, axis=-1, keepdims=True))
    alpha = jnp.exp(m_old - m_new)
    p = jnp.exp(s - m_new)
    l_ref[...] = alpha * l_ref[...] + jnp.sum(p, axis=-1, keepdims=True)
    acc_ref[...] = alpha * acc_ref[...] + _dot(p.astype(BF16), v)
    m_ref[...] = m_new


def _softmax_init(m_ref, l_ref, acc_ref):
    m_ref[...] = jnp.full(m_ref.shape, NEG_INF, F32)
    l_ref[...] = jnp.zeros(l_ref.shape, F32)
    acc_ref[...] = jnp.zeros(acc_ref.shape, F32)


def _attn_kernel(lam_ref, qt_ref, k_ref, vt_ref, band_ref, gd_ref, o_ref, m_ref, l_ref, acc_ref, *, tq):
    i = pl.program_id(2)
    qt = qt_ref[0, 0, 0]
    _softmax_init(m_ref, l_ref, acc_ref)

    def tile(j, bias):
        s = _dot(k_ref[0, j], qt)
        if bias is not None:
            s = s + jnp.concatenate([bias, bias], axis=1)
        m_old = m_ref[...]
        m_new = jnp.maximum(m_old, jnp.max(s, axis=0, keepdims=True))
        alpha = jnp.exp(m_old - m_new)
        p = jnp.exp(s - m_new)
        l_ref[...] = alpha * l_ref[...] + jnp.sum(p, axis=0, keepdims=True)
        acc_ref[...] = alpha * acc_ref[...] + _dot(vt_ref[0, 0, j], p.astype(BF16))
        m_ref[...] = m_new

    def far(j, carry):
        tile(j, None)
        return carry

    lax.fori_loop(0, jnp.maximum(i - 1, 0), far, 0)

    @pl.when(i >= 1)
    def _():
        tile(i - 1, band_ref[0, 1])

    tile(i, band_ref[0, 0])

    o = acc_ref[...] / l_ref[...]
    o = o[:, :tq] - lam_ref[0] * o[:, tq:]
    o = o * lax.rsqrt(jnp.mean(o * o, axis=0, keepdims=True) + RMS_EPS) * gd_ref[...] * (1.0 - LAM_INIT)
    o_ref[0] = o.T.astype(BF16)


def prompt_attention(lam, q2, kb, vb, band_t, gd, batch, seq, tq):
    nq = seq // tq
    head = 2 * DH
    qt = jnp.transpose(q2.reshape(2, batch, nq, tq, H_DIFF, head), (1, 4, 2, 5, 0, 3)).reshape(
        batch, H_DIFF, nq, head, 2 * tq)
    k4 = kb.reshape(batch, nq, tq, D_DIFF)
    vt = jnp.transpose(vb.reshape(batch, nq, tq, H_DIFF, head), (0, 3, 1, 4, 2))
    return pl.pallas_call(
        functools.partial(_attn_kernel, tq=tq),
        grid=(batch, H_DIFF, nq),
        in_specs=[pl.BlockSpec(memory_space=pltpu.SMEM),
                  pl.BlockSpec((1, 1, 1, head, 2 * tq), lambda b, h, i: (b, h, i, 0, 0)),
                  pl.BlockSpec((1, nq, tq, head), lambda b, h, i: (b, 0, 0, h)),
                  pl.BlockSpec((1, 1, nq, head, tq), lambda b, h, i: (b, h, 0, 0, 0)),
                  pl.BlockSpec((1, 2, tq, tq), lambda b, h, i: (h, 0, 0, 0)),
                  _const_spec((head, 1))],
        out_specs=pl.BlockSpec((1, tq, head), lambda b, h, i: (b, i, h)),
        out_shape=jax.ShapeDtypeStruct((batch, seq, D_DIFF), BF16),
        scratch_shapes=[pltpu.VMEM((1, 2 * tq), F32), pltpu.VMEM((1, 2 * tq), F32), pltpu.VMEM((head, 2 * tq), F32)],
        compiler_params=_params("parallel", "parallel", "arbitrary"),
        name="prompt_attention",
    )(lam, qt, k4, vt, band_t, gd.reshape(head, 1))


DEC_ROWS = 2 * H_DIFF


def _dec_attn_kernel(pt_ref, lam_ref, q_ref, *refs, n_groups, n_q, group):
    k_refs, v_refs = refs[:group], refs[group:2 * group]
    kn_ref, vn_ref, band_ref, expand_ref, gd_ref, o_ref, m_ref, l_ref, acc_ref = refs[2 * group:]
    j = pl.program_id(1)
    q = q_ref[0]
    page = band_ref.shape[2]

    @pl.when(j == 0)
    def _():
        _softmax_init(m_ref, l_ref, acc_ref)

    def update(s, weighted_values):
        m_old = m_ref[...]
        m_new = jnp.maximum(m_old, jnp.max(s, axis=-1, keepdims=True))
        alpha = jnp.exp(m_old - m_new)
        p = jnp.exp(s - m_new)
        l_ref[...] = alpha * l_ref[...] + jnp.sum(p, axis=-1, keepdims=True)
        acc_ref[...] = alpha * acc_ref[...] + weighted_values(p.astype(BF16))
        m_ref[...] = m_new

    scores = [_dot(q, k_ref[0].reshape(D_DIFF, page).astype(BF16)) for k_ref in k_refs]
    scores[-1] = scores[-1] + jnp.where(j == n_groups - 1, band_ref[0], 0.0)

    rows = q.shape[0]
    row_head = (lax.broadcasted_iota(I32, (rows, page * H_DIFF), 0) % (H_DIFF * n_q)) // n_q
    own_rows = (lax.broadcasted_iota(I32, (rows, page * H_DIFF), 1) % H_DIFF) == row_head

    def cached_values(p):
        o = None
        for u, v_ref in enumerate(v_refs):
            spread = _dot(p[:, u * page:(u + 1) * page], expand_ref[...])
            part = _dot(jnp.where(own_rows, spread, 0.0).astype(BF16), v_ref[0].astype(BF16))
            o = part if o is None else o + part
        return o

    update(jnp.concatenate(scores, axis=1), cached_values)

    @pl.when(j == n_groups - 1)
    def _():
        def new_values(p):
            full = _dot(p, vn_ref[0])
            head = (lax.broadcasted_iota(I32, (rows, 2 * DH), 0) % (H_DIFF * n_q)) // n_q
            return sum(jnp.where(head == h, full[:, h * 2 * DH:(h + 1) * 2 * DH], 0.0) for h in range(H_DIFF))

        update(_dot_nt(q, kn_ref[0]) + band_ref[1], new_values)
        o = acc_ref[...] / l_ref[...]
        half = rows // 2
        o = o[:half] - lam_ref[0] * o[half:]
        o = _rms(o, gd_ref[...]) * (1.0 - LAM_INIT)
        o_ref[0] = jnp.concatenate([o[h * n_q:(h + 1) * n_q] for h in range(H_DIFF)], axis=1).astype(BF16)


PAGE_GROUP = 8


def sample_attention(page_table, lam, qst, cache_kt, cache_v, kn, vn, band, gd, n_q):
    b, n_pages = page_table.shape
    page = cache_kt.shape[-1]
    rows = n_q * DEC_ROWS
    expand = jnp.repeat(jnp.eye(page, dtype=BF16), H_DIFF, axis=1)
    group = math.gcd(PAGE_GROUP, n_pages)
    n_groups = n_pages // group

    def page_of(u):
        return lambda i, j, pt: (pt[i, j * group + u],)

    k_specs = [pl.BlockSpec((1,) + cache_kt.shape[1:], lambda i, j, pt, f=page_of(u): f(i, j, pt) + (0, 0, 0, 0))
               for u in range(group)]
    v_specs = [pl.BlockSpec((1,) + cache_v.shape[1:], lambda i, j, pt, f=page_of(u): f(i, j, pt) + (0, 0))
               for u in range(group)]
    new_spec = pl.BlockSpec((1, page, D_DIFF), lambda i, j, pt: (i, 0, 0))
    grid_spec = pltpu.PrefetchScalarGridSpec(
        num_scalar_prefetch=1,
        grid=(b, n_groups),
        in_specs=[pl.BlockSpec(memory_space=pltpu.SMEM),
                  pl.BlockSpec((1, rows, D_DIFF), lambda i, j, pt: (i, 0, 0))]
                 + k_specs + v_specs
                 + [new_spec, new_spec,
                    pl.BlockSpec((2, rows, page), lambda i, j, pt: (0, 0, 0)),
                    pl.BlockSpec(expand.shape, lambda i, j, pt: (0, 0)),
                    pl.BlockSpec((1, 2 * DH), lambda i, j, pt: (0, 0))],
        out_specs=pl.BlockSpec((1, n_q, D_DIFF), lambda i, j, pt: (i, 0, 0)),
        scratch_shapes=[pltpu.VMEM((rows, 1), F32), pltpu.VMEM((rows, 1), F32), pltpu.VMEM((rows, 2 * DH), F32)],
    )
    return pl.pallas_call(
        functools.partial(_dec_attn_kernel, n_groups=n_groups, n_q=n_q, group=group),
        grid_spec=grid_spec,
        out_shape=jax.ShapeDtypeStruct((b, n_q, D_DIFF), BF16),
        compiler_params=_params("parallel", "arbitrary"),
        name="sample_attention",
    )(page_table, lam, qst, *([cache_kt] * group), *([cache_v] * group), kn, vn, band, expand, gd)


def _mid_kernel(x_ref, ys_ref, yd_ref, wa_ref, wb_ref, g_ref, wq_ref, h_ref, m_ref, qp_ref):
    h = x_ref[...] + _dot(ys_ref[...], wa_ref[...]) + _dot(yd_ref[...], wb_ref[...])
    h_ref[...] = h
    m = _rms(h, g_ref[...]).astype(BF16)
    mf = m.astype(F32)
    for c in range(D_MODEL // LANES):
        m_ref[:, c, :] = mf[:, c * LANES:(c + 1) * LANES]
    qp_ref[...] = _dot(m, wq_ref[...]).astype(BF16)


def mid(x, ys, yd, wa, wb, g, wq):
    t = x.shape[0]
    tm = min(ROW_TILE, t)
    row = lambda n: pl.BlockSpec((tm, n), lambda i: (i, 0))
    nq = wq.shape[1]
    return pl.pallas_call(
        _mid_kernel,
        grid=(t // tm,),
        in_specs=[row(D_MODEL), row(D_SSM), row(D_DIFF), _const_spec(wa.shape), _const_spec(wb.shape),
                  _const_spec((1, D_MODEL)), _const_spec(wq.shape)],
        out_specs=(row(D_MODEL), pl.BlockSpec((tm, D_MODEL // LANES, LANES), lambda i: (i, 0, 0)), row(nq)),
        out_shape=(jax.ShapeDtypeStruct((t, D_MODEL), F32), jax.ShapeDtypeStruct((t, D_MODEL // LANES, LANES), F32),
                   jax.ShapeDtypeStruct((t, nq), BF16)),
        compiler_params=_params("parallel"),
        name="mid",
    )(x, ys, yd, wa, wb, g, wq)


def _top16(vals):
    n = vals.shape[0]
    from_end = (n - 1 - lax.broadcasted_iota(I32, vals.shape, 0)).astype(F32)
    out_v, out_i = [], []
    cur = vals
    for _ in range(PEER_TOPK):
        mx = jnp.max(cur, axis=0, keepdims=True)
        at = jnp.max(jnp.where(cur == mx, from_end, -1.0), axis=0, keepdims=True)
        out_v.append(mx)
        out_i.append(at)
        cur = jnp.where(from_end == at, -jnp.inf, cur)
    return jnp.concatenate(out_v, axis=0), n - 1 - jnp.concatenate(out_i, axis=0).astype(I32)


HI_MASK = -65536
LOW_HALF_SHIFT = 16


def _topk_kernel(qp_ref, sk_ref, row_ref, shift_ref, shift_t_ref, gate_ref):
    tb = qp_ref.shape[0]
    half = D_KEY // 2
    idx_rows, gate_rows = [], []
    for h in range(PEER_HEADS):
        s, ix = [], []
        for c in range(2):
            qh = qp_ref[:, (2 * h + c) * half:(2 * h + c + 1) * half]
            sc = _dot_nt(sk_ref[c], qh)
            v, i = _top16(sc)
            s.append(v)
            ix.append(i)
        pieces = [(a, PEER_TOPK if a == 0 else SUBLANES) for a in range(SUBLANES)]
        cand = jnp.concatenate([s[0][a:a + 1, :] + s[1][:nb] for a, nb in pieces] + [s[0][SUBLANES:] + s[1][0:1]],
                               axis=0)
        cidx = jnp.concatenate([ix[0][a:a + 1, :] * N_KEYS + ix[1][:nb] for a, nb in pieces]
                               + [ix[0][SUBLANES:] * N_KEYS + ix[1][0:1]], axis=0)
        sc, pos = _top16(cand)
        prow = lax.broadcasted_iota(I32, cand.shape, 0)
        cidx_f = cidx.astype(F32)
        idx = jnp.concatenate(
            [jnp.max(jnp.where(prow == pos[k:k + 1, :], cidx_f, -1.0), axis=0, keepdims=True)
             for k in range(PEER_TOPK)], axis=0).astype(I32)
        e = jnp.exp(sc - sc[0:1, :])
        gate_rows.append(e / jnp.sum(e, axis=0, keepdims=True))
        idx_rows.append(idx)
    idx_t = jnp.concatenate(idx_rows, axis=0)
    shift_t = ((idx_t & 1) ^ 1) * LOW_HALF_SHIFT
    shift_t_ref[0] = shift_t.astype(F32)
    idx = idx_t.T
    gate_ref[...] = jnp.concatenate(gate_rows, axis=0).T
    row_ref[...] = (idx >> 1) * SUBLANES
    shift_ref[...] = shift_t.T


def peer_topk(qp, sk):
    t = qp.shape[0]
    tb = min(PEER_TILE, t)
    out = pl.BlockSpec((tb, SLOTS), lambda i: (i, 0))
    return pl.pallas_call(
        _topk_kernel,
        grid=(t // tb,),
        in_specs=[pl.BlockSpec((tb, qp.shape[1]), lambda i: (i, 0)), _const_spec(sk.shape)],
        out_specs=(out, out, pl.BlockSpec((1, SLOTS, tb), lambda i: (i, 0, 0)), out),
        out_shape=(jax.ShapeDtypeStruct((t, SLOTS), I32), jax.ShapeDtypeStruct((t, SLOTS), I32),
                   jax.ShapeDtypeStruct((t // tb, SLOTS, tb), F32), jax.ShapeDtypeStruct((t, SLOTS), F32)),
        compiler_params=_params("parallel"),
        name="peer_topk",
    )(qp, sk)


def _pack_kernel(w_ref, o_ref):
    n = o_ref.shape[0]
    even = w_ref[pl.ds(0, n, stride=2), :].astype(BF16).astype(F32)
    odd = w_ref[pl.ds(1, n, stride=2), :].astype(BF16).astype(F32)
    lo = (lax.bitcast_convert_type(even, I32) >> 16) & 0xFFFF
    o_ref[...] = lo | (lax.bitcast_convert_type(odd, I32) & HI_MASK)


def pack_table(w):
    n, d = w.shape
    rows = 2048
    packed = pl.pallas_call(
        _pack_kernel,
        grid=(n // (2 * rows), d // LANES),
        in_specs=[pl.BlockSpec((2 * rows, LANES), lambda i, c: (i, c))],
        out_specs=pl.BlockSpec((rows, LANES), lambda i, c: (i, c)),
        out_shape=jax.ShapeDtypeStruct((n // 2, d), I32),
        compiler_params=_params("parallel", "parallel"),
        name="pack_table",
    )(w)
    return packed.reshape(n // 2 * (d // LANES), LANES)


def _expert_row(tab_ref, row, shift):
    word = tab_ref[pl.ds(pl.multiple_of(row, SUBLANES), SUBLANES), :]
    return lax.bitcast_convert_type((word << shift) & HI_MASK, F32)


UNIT = SUBLANES


def _butterfly(parts, add, roll, select):
    step = SUBLANES // 2
    while len(parts) > 1:
        parts = [select(step, add(a, roll(a, step)), add(b, roll(b, SUBLANES - step)))
                 for a, b in zip(parts[0::2], parts[1::2])]
        step //= 2
    return parts[0]


def _butterfly_order():
    tiles = [[{p}] * SUBLANES for p in range(SUBLANES)]
    out = _butterfly(
        tiles,
        add=lambda a, b: [x | y for x, y in zip(a, b)],
        roll=lambda a, k: [a[(i - k) % SUBLANES] for i in range(SUBLANES)],
        select=lambda step, a, b: [a[i] if (i // step) % 2 else b[i] for i in range(SUBLANES)])
    order = [0] * SUBLANES
    for sublane, owners in enumerate(out):
        (p,) = owners
        order[p] = sublane
    return order


BUTTERFLY_ORDER = _butterfly_order()


def _rows_to_sublanes(parts):
    sub = lax.broadcasted_iota(I32, (SUBLANES, LANES), 0)
    placed = [None] * SUBLANES
    for i, part in enumerate(parts):
        placed[BUTTERFLY_ORDER.index(i)] = part
    return _butterfly(placed, add=lambda a, b: a + b, roll=lambda a, k: pltpu.roll(a, k, 0),
                      select=lambda step, a, b: jnp.where((sub // step) % 2 == 1, a, b))


def _splat(ref, j, t):
    return jnp.broadcast_to(ref[j:j + 1, t * LANES:(t + 1) * LANES], (SUBLANES, LANES))


def _peer_u_kernel(row_ref, m_ref, gate_ref, shiftv_ref, shift_t_ref, spread_ref, tab_ref, wt_ref,
                   part_ref, h_ref, pad_ref, shift_splat_ref):
    tb = m_ref.shape[0]
    step = pl.program_id(0)

    @pl.when(step == 0)
    def _():
        pad_ref[...] = jnp.zeros(pad_ref.shape, F32)

    picks = spread_ref[step % spread_ref.shape[0]]
    shift_splat_ref[...] = _dot(shift_t_ref[0].astype(BF16), picks).astype(I32)

    def token(t):
        x = m_ref[t]
        for u in range(SLOTS // UNIT):
            parts = [_expert_row(tab_ref, row_ref[t, j], _splat(shift_splat_ref, j, t)) * x
                     for j in range(u * UNIT, (u + 1) * UNIT)]
            part_ref[t, u * UNIT:(u + 1) * UNIT, :] = _rows_to_sublanes(parts)

    for t in range(tb):
        token(t)
    for t in range(tb):
        h_ref[t:t + 1, :] = jnp.sum(part_ref[t].T, axis=0, keepdims=True)
    h = h_ref[...]
    act = 0.5 * h * (1.0 + lax.erf(h * (2.0 ** -0.5)))
    pad_ref[0:tb, :] = (gate_ref[...] * act).astype(BF16).astype(F32)
    pad_ref[tb:2 * tb, :] = shiftv_ref[...].astype(F32)
    wt_ref[0] = pad_ref[...].T


def _token_picks(n_tokens, tb):
    eye = jnp.eye(n_tokens, dtype=BF16).reshape(n_tokens, n_tokens // tb, tb)
    return jnp.repeat(jnp.moveaxis(eye, 1, 0), LANES, axis=2)


def peer_u(row, shift, shift_t, m3, gate, tab):
    t = m3.shape[0]
    tb = min(EXPERT_TILE, t)
    top_tb = shift_t.shape[2]
    per_top = top_tb // tb
    picks = _token_picks(top_tb, tb)
    smem = pl.BlockSpec((tb, SLOTS), lambda i: (i, 0), memory_space=pltpu.SMEM)
    vmem = pl.BlockSpec((tb, SLOTS), lambda i: (i, 0))
    return pl.pallas_call(
        _peer_u_kernel,
        grid=(t // tb,),
        in_specs=[smem, pl.BlockSpec((tb, SUBLANES, LANES), lambda i: (i, 0, 0)), vmem, vmem,
                  pl.BlockSpec((1, SLOTS, top_tb), lambda i: (i // per_top, 0, 0)), _const_spec(picks.shape),
                  pl.BlockSpec(tab.shape, lambda i: (0, 0), pipeline_mode=pl.Buffered(1))],
        out_specs=pl.BlockSpec((1, SLOTS, LANES), lambda i: (i, 0, 0)),
        out_shape=jax.ShapeDtypeStruct((t // tb, SLOTS, LANES), F32),
        scratch_shapes=[pltpu.VMEM((tb, SLOTS, LANES), F32), pltpu.VMEM((tb, SLOTS), F32),
                        pltpu.VMEM((LANES, SLOTS), F32), pltpu.VMEM((SLOTS, tb * LANES), I32)],
        compiler_params=_params("arbitrary"),
        name="peer_u",
    )(row, m3, gate, shift, shift_t, picks, tab)


N_ACC = 8


REGION_TOKENS = 4


def _token_regions(n, body):
    always = pl.program_id(0) >= 0

    def region(first):
        for t in range(first, min(first + REGION_TOKENS, n)):
            body(t)

    for first in range(0, n, REGION_TOKENS):
        pl.when(always)(functools.partial(region, first))


def _peer_v_kernel(row_ref, wt_ref, spread_ref, tab_ref, o_ref, code_ref):
    tb = o_ref.shape[0]
    spread = _dot(wt_ref[0][:, :2 * tb].astype(BF16), spread_ref[...])
    code_ref[...] = (lax.bitcast_convert_type(spread[:, :tb * LANES], I32)
                     | spread[:, tb * LANES:].astype(I32))

    def token(t):
        acc = [None] * N_ACC
        for j in range(SLOTS):
            code = _splat(code_ref, j, t)
            weight = lax.bitcast_convert_type(code & HI_MASK, F32)
            term = weight * _expert_row(tab_ref, row_ref[t, j], code & LOW_HALF_SHIFT)
            acc[j % N_ACC] = term if acc[j % N_ACC] is None else acc[j % N_ACC] + term
        while len(acc) > 1:
            acc = [a + b for a, b in zip(acc[0::2], acc[1::2])]
        o_ref[t] = acc[0]

    _token_regions(tb, token)


def peer_v(row, wt, tab):
    t = row.shape[0]
    tb = min(EXPERT_TILE, t)
    smem = pl.BlockSpec((tb, SLOTS), lambda i: (i, 0), memory_space=pltpu.SMEM)
    spread = jnp.repeat(jnp.eye(2 * tb, dtype=BF16), LANES, axis=1)
    return pl.pallas_call(
        _peer_v_kernel,
        grid=(t // tb,),
        in_specs=[smem, pl.BlockSpec((1, SLOTS, LANES), lambda i: (i, 0, 0)), _const_spec(spread.shape),
                  pl.BlockSpec(tab.shape, lambda i: (0, 0), pipeline_mode=pl.Buffered(1))],
        out_specs=pl.BlockSpec((tb, SUBLANES, LANES), lambda i: (i, 0, 0)),
        out_shape=jax.ShapeDtypeStruct((t, SUBLANES, LANES), F32),
        scratch_shapes=[pltpu.VMEM((SLOTS, tb * LANES), I32)],
        compiler_params=_params("arbitrary"),
        name="peer_v",
    )(row, wt, spread, tab)


def _ple_kernel(h_ref, peer_ref, p_ref, wp_ref, wg_ref, gp_ref, gf_ref, y_ref):
    peer = jnp.concatenate([peer_ref[:, c, :] for c in range(D_MODEL // LANES)], axis=1)
    h = h_ref[...] + peer
    gate = jax.nn.sigmoid(_dot(_rms(h, gp_ref[...]).astype(BF16), wg_ref[...]))
    h = h + _dot(p_ref[...].astype(BF16), wp_ref[...]) * gate
    y_ref[...] = _rms(h, gf_ref[...])


def ple(h, peer_out, p, wp, wg, gp, gf):
    t = h.shape[0]
    tm = min(ROW_TILE, t)
    row = lambda n: pl.BlockSpec((tm, n), lambda i: (i, 0))
    return pl.pallas_call(
        _ple_kernel,
        grid=(t // tm,),
        in_specs=[row(D_MODEL), pl.BlockSpec((tm, D_MODEL // LANES, LANES), lambda i: (i, 0, 0)), row(PLE_DIM),
                  _const_spec(wp.shape), _const_spec(wg.shape),
                  _const_spec((1, D_MODEL)), _const_spec((1, D_MODEL))],
        out_specs=row(D_MODEL),
        out_shape=jax.ShapeDtypeStruct((t, D_MODEL), F32),
        compiler_params=_params("parallel"),
        name="ple",
    )(h, peer_out, p, wp, wg, gp, gf)


def _t5_bucket(rel):
    n = jnp.maximum(rel, 0)
    max_exact = N_BUCKETS // 2
    nf = jnp.maximum(n, 1).astype(F32)
    large = max_exact + (jnp.log(nf / max_exact) / math.log(REL_MAX_DIST / max_exact)
                         * (N_BUCKETS - max_exact)).astype(I32)
    large = jnp.minimum(large, N_BUCKETS - 1)
    return jnp.where(n < max_exact, n, large)


def _band_bias(rel_bias, rel):
    far = rel_bias[N_BUCKETS - 1]
    bucket = _t5_bucket(rel)[..., None]
    b = jnp.zeros(rel.shape + (H_DIFF,), F32)
    for n in range(N_BUCKETS - 1):
        b = jnp.where(bucket == n, rel_bias[n] - far, b)
    b = jnp.where((rel >= 0)[..., None], b, NEG_INF)
    return jnp.moveaxis(b, -1, 0).astype(F32)


def _pad_rows(a, rows):
    return jnp.pad(a, ((0, 0), (0, rows - a.shape[1]), (0, 0)))


def _layer(x, p_l, conv_buf, ssm0, lam, w, attention):
    b, l, _ = x.shape
    t = b * l
    x2 = x.reshape(t, D_MODEL)
    z, xbc, q2, k, v, kb, vb, dt = in_proj(x2, w["g_mix"], w["wz"], w["wx"], w["wq"], w["wk"], w["wv"], w["wdt"], l)
    if k.ndim == 3:
        k = jnp.transpose(k.reshape(b, H_DIFF, 2, DH, l), (0, 4, 1, 2, 3))
    else:
        k = k.reshape(b, l, H_DIFF, 2, DH)

    lp = -(-l // SSD_CHUNK) * SSD_CHUNK
    xbc3 = xbc.reshape(b, l, CONV_DIM)
    cbuf = jnp.pad(conv_buf, ((0, 0), (SUBLANES - (CONV_W - 1), 0), (0, 0)))
    y_ssm, ssm_final = ssd(_pad_rows(xbc3, lp), _pad_rows(z.reshape(b, l, D_SSM), lp),
                           _pad_rows(dt.reshape(b, l, LANES), lp), cbuf, ssm0,
                           w["conv_w"], w["conv_b"], w["dt_bias"], w["a_log"], w["d_skip"], w["g_ssm_norm"], l)
    y_ssm = y_ssm[:, :l].reshape(t, D_SSM)
    tail = CONV_W - 1
    conv_new = xbc3[:, l - tail:] if l >= tail else jnp.concatenate([conv_buf, xbc3], axis=1)[:, -tail:]

    y_diff = attention(q2, kb, vb)

    h1, m, qp = mid(x2, y_ssm, y_diff, w["wo_a"], w["wo_b"], w["g_ffn"], w["w_query"])
    row, shift, shift_t, gate = peer_topk(qp, w["sub_keys"])
    wt = peer_u(row, shift, shift_t, m, gate, w["tab_u"])
    peer_out = peer_v(row, wt, w["tab_v"])
    y = ple(h1, peer_out, p_l.reshape(t, PLE_DIM), w["w_ple"], w["w_ple_gate"], w["g_ple"], w["g_final"])
    return (y.reshape(b, l, D_MODEL), k, v.reshape(b, l, H_DIFF, 2 * DH),
            ssm_final, conv_new)


def kernel(x_prompt, x_sample, cache_k, cache_v, state_ssm, state_conv, page_table, p_prompt, p_sample, g_mix, w_in, conv_w, conv_b, dt_bias, a_log, d_skip, g_ssm_norm, lam_q1, lam_k1, lam_q2, lam_k2, g_diff_norm, rel_bias, w_out, g_ffn, w_query, sub_keys, expert_u, expert_v, w_ple, w_ple_gate, g_ple, g_final):
    assert w_in.shape[0] == 1, "single-layer problem"
    bsz, seq, _ = x_prompt.shape
    dec_b, dec_l, _ = x_sample.shape
    n_pages = page_table.shape[1]
    page = cache_k.shape[2]
    past = n_pages * page

    wi = w_in[0].astype(BF16)
    o_x = D_SSM
    o_dt = o_x + CONV_DIM
    o_q = o_dt + H_SSM
    o_k = o_q + D_DIFF
    o_v = o_k + D_DIFF
    lanes_of = lambda a: jnp.pad(a.astype(F32)[None, :], ((0, 0), (0, LANES - a.shape[0])))
    w = dict(
        g_mix=g_mix[0][None], wz=wi[:, :o_x], wx=wi[:, o_x:o_dt],
        wdt=jnp.pad(wi[:, o_dt:o_q], ((0, 0), (0, LANES - H_SSM))),
        wq=wi[:, o_q:o_k], wk=wi[:, o_k:o_v], wv=wi[:, o_v:],
        conv_w=conv_w[0], conv_b=conv_b[0][None], dt_bias=lanes_of(dt_bias[0]), a_log=lanes_of(a_log[0]),
        d_skip=lanes_of(d_skip[0]), g_ssm_norm=g_ssm_norm[0][None],
        wo_a=w_out[0][:D_SSM].astype(BF16), wo_b=w_out[0][D_SSM:].astype(BF16), g_ffn=g_ffn[0][None],
        w_query=w_query[0].astype(BF16), sub_keys=sub_keys[0].astype(BF16),
        tab_u=pack_table(expert_u[0]), tab_v=pack_table(expert_v[0]),
        w_ple=w_ple[0].astype(BF16), w_ple_gate=w_ple_gate[0].astype(BF16), g_ple=g_ple[0][None],
        g_final=g_final[None],
    )
    gd = g_diff_norm[0][None]
    lam = (jnp.exp(jnp.sum(lam_q1[0] * lam_k1[0])) - jnp.exp(jnp.sum(lam_q2[0] * lam_k2[0])) + LAM_INIT).reshape(1)

    tq = min(ATTN_TILE, seq)
    ii = jnp.arange(tq, dtype=I32)
    assert tq >= REL_MAX_DIST and page >= REL_MAX_DIST, "bias is only applied next to the diagonal"
    rel = jnp.stack([ii[None, :] - ii[:, None], tq + ii[None, :] - ii[:, None]])
    band_p = _band_bias(rel_bias, rel)

    def attn_prompt(q2, kb, vb):
        return prompt_attention(lam, q2, kb, vb, band_p, gd, bsz, seq, tq).reshape(bsz * seq, D_DIFF)

    yp, kp, vp, sp, cp = _layer(x_prompt, p_prompt[0], jnp.zeros((bsz, CONV_W - 1, CONV_DIM), F32),
                                jnp.zeros((bsz, H_SSM, P_SSM, N_STATE), F32), lam, w, attn_prompt)

    rows = dec_l * DEC_ROWS
    r = jnp.arange(rows, dtype=I32)
    qpos = past + r % dec_l
    head_of = (r % (H_DIFF * dec_l)) // dec_l
    map_of = r // (H_DIFF * dec_l)
    kk = jnp.arange(page, dtype=I32)
    rel_last = qpos[:, None] - (past - page + kk)[None, :]
    rel_new = jnp.where(kk[None, :] < dec_l, qpos[:, None] - (past + kk)[None, :], -1)
    band_all = _band_bias(rel_bias, jnp.stack([rel_last, rel_new]))
    band_s = jnp.take_along_axis(band_all, head_of[None, None, :, None], axis=0)[0]
    lane_slot = jnp.arange(D_DIFF, dtype=I32) // DH
    q_mask = (lane_slot[None, :] == (head_of * 2 + map_of)[:, None])
    cache_v2 = cache_v[0].reshape(-1, page * H_DIFF, 2 * DH)

    def attn_sample(q2, kb, vb):
        q = (q2[0] + q2[1]).reshape(dec_b, dec_l, D_DIFF)
        qst = jnp.where(q_mask[None], jnp.tile(q, (1, DEC_ROWS, 1)), jnp.zeros((), BF16))
        kn = _pad_rows(kb.reshape(dec_b, dec_l, D_DIFF), page)
        vn = _pad_rows(vb.reshape(dec_b, dec_l, D_DIFF), page)
        o = sample_attention(page_table, lam, qst, jnp.transpose(cache_k[0], (0, 2, 3, 4, 1)), cache_v2,
                             kn, vn, band_s, gd, dec_l)
        return o.reshape(dec_b * dec_l, D_DIFF)

    ys, ks, vs, ss, cs = _layer(x_sample, p_sample[0], state_conv[0], state_ssm[0], lam, w, attn_sample)

    return (yp, ys, kp[None], vp[None], sp[None], cp[None], ks[None], vs[None], ss[None], cs[None])
```

```python
import functools
import math

import jax
import jax.numpy as jnp
from jax import lax
from jax.experimental import pallas as pl
from jax.experimental.pallas import tpu as pltpu

F32 = jnp.float32
BF16 = jnp.bfloat16
I32 = jnp.int32

D_MODEL = 1024
D_SSM = 512
D_DIFF = 512
P_SSM = 64
H_SSM = 8
N_STATE = 128
SSM_GROUPS = 2
HEADS_PER_GROUP = H_SSM // SSM_GROUPS
CONV_W = 4
CONV_DIM = D_SSM + 2 * SSM_GROUPS * N_STATE
DH = 64
H_DIFF = 4
N_BUCKETS = 32
REL_MAX_DIST = 128
N_KEYS = 128
PEER_HEADS = 8
PEER_TOPK = 16
D_KEY = 256
PLE_DIM = 256
NEG_INF = -1e30
RMS_EPS = 1e-6
LAM_INIT = 0.8 - 0.6 * math.exp(-0.3 * 0)

LANES = 128
SUBLANES = 8
VMEM_LIMIT = 56 * 1024 * 1024

SSD_CHUNK = 128
ROW_TILE = 256
ATTN_TILE = 1024
PEER_TILE = 128
EXPERT_TILE = 32
SLOTS = PEER_HEADS * PEER_TOPK


def _params(*sem):
    return pltpu.CompilerParams(dimension_semantics=sem, vmem_limit_bytes=VMEM_LIMIT)


def _const_spec(shape):
    nd = len(shape)
    return pl.BlockSpec(shape, lambda *_: (0,) * nd)


def _rms(x, g):
    return x * lax.rsqrt(jnp.mean(x * x, axis=-1, keepdims=True) + RMS_EPS) * g


def _dot(a, b):
    return jnp.dot(a, b, preferred_element_type=F32)


def _dot_nt(a, b):
    return lax.dot_general(a, b, (((1,), (1,)), ((), ())), preferred_element_type=F32)


def _dot_tn(a, b):
    return lax.dot_general(a, b, (((0,), (0,)), ((), ())), preferred_element_type=F32)


def _in_proj_kernel(x_ref, g_ref, wz_ref, wx_ref, wq_ref, wk_ref, wv_ref, wdt_ref,
                    z_ref, xbc_ref, q2_ref, k_ref, v_ref, kb_ref, vb_ref, dt_ref, *, transposed_k):
    a = _rms(x_ref[...], g_ref[...]).astype(BF16)
    z_ref[...] = _dot(a, wz_ref[...])
    xbc_ref[...] = _dot(a, wx_ref[...])
    dt_ref[...] = _dot(a, wdt_ref[...])
    k = _dot(a, wk_ref[...])
    v = _dot(a, wv_ref[...])
    if transposed_k:
        k_ref[0] = k.T
    else:
        k_ref[...] = k
    for h in range(H_DIFF):
        v_ref[:, h, :] = v[:, h * 2 * DH:(h + 1) * 2 * DH]
    kb_ref[...] = k.astype(BF16)
    vb_ref[...] = v.astype(BF16)
    q = _dot(a, wq_ref[...]).astype(BF16) * (DH ** -0.5)
    lane = lax.broadcasted_iota(I32, q.shape, 1)
    first_map = (lane % (2 * DH)) < DH
    zero = jnp.zeros_like(q)
    q2_ref[0] = jnp.where(first_map, q, zero)
    q2_ref[1] = jnp.where(first_map, zero, q)


def in_proj(x, g, wz, wx, wq, wk, wv, wdt, seq):
    t = x.shape[0]
    tm = min(ROW_TILE, t)
    row = lambda n: pl.BlockSpec((tm, n), lambda i: (i, 0))
    transposed_k = seq % tm == 0
    per_seq = seq // tm if transposed_k else 1
    if transposed_k:
        k_shape = jax.ShapeDtypeStruct((t // seq, D_DIFF, seq), F32)
        k_spec = pl.BlockSpec((1, D_DIFF, tm), lambda i: (i // per_seq, 0, i % per_seq))
    else:
        k_shape, k_spec = jax.ShapeDtypeStruct((t, D_DIFF), F32), row(D_DIFF)
    out_shape = (
        jax.ShapeDtypeStruct((t, D_SSM), F32),
        jax.ShapeDtypeStruct((t, CONV_DIM), F32),
        jax.ShapeDtypeStruct((2, t, D_DIFF), BF16),
        k_shape,
        jax.ShapeDtypeStruct((t, H_DIFF, 2 * DH), F32),
        jax.ShapeDtypeStruct((t, D_DIFF), BF16),
        jax.ShapeDtypeStruct((t, D_DIFF), BF16),
        jax.ShapeDtypeStruct((t, LANES), F32),
    )
    return pl.pallas_call(
        functools.partial(_in_proj_kernel, transposed_k=transposed_k),
        grid=(t // tm,),
        in_specs=[row(D_MODEL), _const_spec((1, D_MODEL)), _const_spec(wz.shape), _const_spec(wx.shape),
                  _const_spec(wq.shape), _const_spec(wk.shape), _const_spec(wv.shape), _const_spec(wdt.shape)],
        out_specs=(row(D_SSM), row(CONV_DIM), pl.BlockSpec((2, tm, D_DIFF), lambda i: (0, i, 0)),
                   k_spec, pl.BlockSpec((tm, H_DIFF, 2 * DH), lambda i: (i, 0, 0)), row(D_DIFF), row(D_DIFF),
                   row(LANES)),
        out_shape=out_shape,
        compiler_params=_params("parallel"),
        name="in_proj",
    )(x, g, wz, wx, wq, wk, wv, wdt)


def _ssd_kernel(xbc_ref, z_ref, dt_ref, cbuf_ref, s0_ref, cw_ref, cb_ref, dtb_ref, alog_ref, dskip_ref, gn_ref,
                y_ref, sfin_ref, win_ref, state_ref, *, seq_len):
    c = pl.program_id(1)
    lc = SSD_CHUNK

    @pl.when(c == 0)
    def _():
        win_ref[0:SUBLANES, :] = cbuf_ref[0]
        state_ref[...] = s0_ref[0]

    win_ref[SUBLANES:SUBLANES + lc, :] = xbc_ref[0]
    base = SUBLANES - (CONV_W - 1)
    conv = cb_ref[...]
    for j in range(CONV_W):
        conv = conv + cw_ref[j:j + 1, :] * win_ref[base + j:base + j + lc, :]
    win_ref[0:SUBLANES, :] = win_ref[lc:lc + SUBLANES, :]
    xc = conv * jax.nn.sigmoid(conv)

    pos = c * lc + lax.broadcasted_iota(I32, (lc, LANES), 0)
    dt = jax.nn.softplus(dt_ref[0] + dtb_ref[...])
    dt = jnp.where(pos < seq_len, dt, 0.0)
    adt = -jnp.exp(alog_ref[...]) * dt
    row = lax.broadcasted_iota(I32, (lc, lc), 0)
    col = lax.broadcasted_iota(I32, (lc, lc), 1)
    causal = row >= col
    tri = causal.astype(F32)
    cs = jnp.dot(tri, adt, preferred_element_type=F32, precision=lax.Precision.HIGHEST)
    cs_t = cs.T
    cs_last = cs[lc - 1:lc, :]

    ys = []
    for g in range(SSM_GROUPS):
        b0 = D_SSM + g * N_STATE
        c0 = D_SSM + SSM_GROUPS * N_STATE + g * N_STATE
        bg = xc[:, b0:b0 + N_STATE]
        cg = xc[:, c0:c0 + N_STATE].astype(BF16)
        cb = _dot_nt(cg, bg.astype(BF16))
        for hh in range(HEADS_PER_GROUP):
            h = g * HEADS_PER_GROUP + hh
            xs = xc[:, h * P_SSM:(h + 1) * P_SSM]
            cs_h = cs[:, h:h + 1]
            decay = jnp.where(causal, jnp.exp(jnp.where(causal, cs_h - cs_t[h:h + 1, :], 0.0)), 0.0)
            xdt = xs * dt[:, h:h + 1]
            y = _dot((cb * decay).astype(BF16), xdt.astype(BF16))
            st = state_ref[h]
            y = y + jnp.exp(cs_h) * _dot_nt(cg, st.astype(BF16))
            y = y + dskip_ref[:, h:h + 1] * xs
            ys.append(y)
            to_end = jnp.exp(cs_last[:, h:h + 1] - cs_h)
            new = _dot_tn((xdt * to_end).astype(BF16), bg.astype(BF16))
            state_ref[h] = jnp.exp(cs_last[:, h:h + 1]) * st + new
    y = jnp.concatenate(ys, axis=1)
    zz = z_ref[0]
    y = y * (zz * jax.nn.sigmoid(zz))
    gw = D_SSM // SSM_GROUPS
    outs = []
    for g in range(SSM_GROUPS):
        yg = y[:, g * gw:(g + 1) * gw]
        outs.append(yg * lax.rsqrt(jnp.mean(yg * yg, axis=-1, keepdims=True) + RMS_EPS))
    y_ref[0] = (jnp.concatenate(outs, axis=1) * gn_ref[...]).astype(BF16)

    @pl.when(c == pl.num_programs(1) - 1)
    def _():
        sfin_ref[0] = state_ref[...]


def ssd(xbc, z, dt, cbuf, s0, cw, cb, dtb, alog, dskip, gn, seq_len):
    b, lp, _ = xbc.shape
    lc = SSD_CHUNK
    seq = lambda n: pl.BlockSpec((1, lc, n), lambda i, c: (i, c, 0))
    return pl.pallas_call(
        functools.partial(_ssd_kernel, seq_len=seq_len),
        grid=(b, lp // lc),
        in_specs=[seq(CONV_DIM), seq(D_SSM), seq(LANES),
                  pl.BlockSpec((1, SUBLANES, CONV_DIM), lambda i, c: (i, 0, 0)),
                  pl.BlockSpec((1, H_SSM, P_SSM, N_STATE), lambda i, c: (i, 0, 0, 0)),
                  _const_spec(cw.shape), _const_spec(cb.shape), _const_spec(dtb.shape), _const_spec(alog.shape),
                  _const_spec(dskip.shape), _const_spec(gn.shape)],
        out_specs=(seq(D_SSM), pl.BlockSpec((1, H_SSM, P_SSM, N_STATE), lambda i, c: (i, 0, 0, 0))),
        out_shape=(jax.ShapeDtypeStruct((b, lp, D_SSM), BF16),
                   jax.ShapeDtypeStruct((b, H_SSM, P_SSM, N_STATE), F32)),
        scratch_shapes=[pltpu.VMEM((SUBLANES + lc, CONV_DIM), F32), pltpu.VMEM((H_SSM, P_SSM, N_STATE), F32)],
        compiler_params=_params("parallel", "arbitrary"),
        name="ssd",
    )(xbc, z, dt, cbuf, s0, cw, cb, dtb, alog, dskip, gn)


def _softmax_tile(q, k, v, bias, m_ref, l_ref, acc_ref):
    s = _dot_nt(q, k)
    if bias is not None:
        s = s + bias
    m_old = m_ref[...]
    m_new = jnp.maximum(m_old, jnp.max(s, axis=-1, keepdims=True))
    alpha = jnp.exp(m_old - m_new)
    p = jnp.exp(s - m_new)
    l_ref[...] = alpha * l_ref[...] + jnp.sum(p, axis=-1, keepdims=True)
    acc_ref[...] = alpha * acc_ref[...] + _dot(p.astype(BF16), v)
    m_ref[...] = m_new


def _softmax_init(m_ref, l_ref, acc_ref):
    m_ref[...] = jnp.full(m_ref.shape, NEG_INF, F32)
    l_ref[...] = jnp.zeros(l_ref.shape, F32)
    acc_ref[...] = jnp.zeros(acc_ref.shape, F32)


def _attn_kernel(lam_ref, qt_ref, k_ref, vt_ref, band_ref, gd_ref, o_ref, m_ref, l_ref, acc_ref, *, tq):
    i = pl.program_id(2)
    qt = qt_ref[0, 0, 0]
    _softmax_init(m_ref, l_ref, acc_ref)

    def tile(j, bias):
        s = _dot(k_ref[0, j], qt)
        if bias is not None:
            s = s + jnp.concatenate([bias, bias], axis=1)
        m_old = m_ref[...]
        m_new = jnp.maximum(m_old, jnp.max(s, axis=0, keepdims=True))
        alpha = jnp.exp(m_old - m_new)
        p = jnp.exp(s - m_new)
        l_ref[...] = alpha * l_ref[...] + jnp.sum(p, axis=0, keepdims=True)
        acc_ref[...] = alpha * acc_ref[...] + _dot(vt_ref[0, 0, j], p.astype(BF16))
        m_ref[...] = m_new

    def far(j, carry):
        tile(j, None)
        return carry

    lax.fori_loop(0, jnp.maximum(i - 1, 0), far, 0)

    @pl.when(i >= 1)
    def _():
        tile(i - 1, band_ref[0, 1])

    tile(i, band_ref[0, 0])

    o = acc_ref[...] / l_ref[...]
    o = o[:, :tq] - lam_ref[0] * o[:, tq:]
    o = o * lax.rsqrt(jnp.mean(o * o, axis=0, keepdims=True) + RMS_EPS) * gd_ref[...] * (1.0 - LAM_INIT)
    o_ref[0] = o.T.astype(BF16)


def prompt_attention(lam, q2, kb, vb, band_t, gd, batch, seq, tq):
    nq = seq // tq
    head = 2 * DH
    qt = jnp.transpose(q2.reshape(2, batch, nq, tq, H_DIFF, head), (1, 4, 2, 5, 0, 3)).reshape(
        batch, H_DIFF, nq, head, 2 * tq)
    k4 = kb.reshape(batch, nq, tq, D_DIFF)
    vt = jnp.transpose(vb.reshape(batch, nq, tq, H_DIFF, head), (0, 3, 1, 4, 2))
    return pl.pallas_call(
        functools.partial(_attn_kernel, tq=tq),
        grid=(batch, H_DIFF, nq),
        in_specs=[pl.BlockSpec(memory_space=pltpu.SMEM),
                  pl.BlockSpec((1, 1, 1, head, 2 * tq), lambda b, h, i: (b, h, i, 0, 0)),
                  pl.BlockSpec((1, nq, tq, head), lambda b, h, i: (b, 0, 0, h)),
                  pl.BlockSpec((1, 1, nq, head, tq), lambda b, h, i: (b, h, 0, 0, 0)),
                  pl.BlockSpec((1, 2, tq, tq), lambda b, h, i: (h, 0, 0, 0)),
                  _const_spec((head, 1))],
        out_specs=pl.BlockSpec((1, tq, head), lambda b, h, i: (b, i, h)),
        out_shape=jax.ShapeDtypeStruct((batch, seq, D_DIFF), BF16),
        scratch_shapes=[pltpu.VMEM((1, 2 * tq), F32), pltpu.VMEM((1, 2 * tq), F32), pltpu.VMEM((head, 2 * tq), F32)],
        compiler_params=_params("parallel", "parallel", "arbitrary"),
        name="prompt_attention",
    )(lam, qt, k4, vt, band_t, gd.reshape(head, 1))


DEC_ROWS = 2 * H_DIFF


def _dec_attn_kernel(pt_ref, lam_ref, q_ref, *refs, n_groups, n_q, group):
    k_refs, v_refs = refs[:group], refs[group:2 * group]
    kn_ref, vn_ref, band_ref, expand_ref, gd_ref, o_ref, m_ref, l_ref, acc_ref = refs[2 * group:]
    j = pl.program_id(1)
    q = q_ref[0]
    page = band_ref.shape[2]

    @pl.when(j == 0)
    def _():
        _softmax_init(m_ref, l_ref, acc_ref)

    def update(s, weighted_values):
        m_old = m_ref[...]
        m_new = jnp.maximum(m_old, jnp.max(s, axis=-1, keepdims=True))
        alpha = jnp.exp(m_old - m_new)
        p = jnp.exp(s - m_new)
        l_ref[...] = alpha * l_ref[...] + jnp.sum(p, axis=-1, keepdims=True)
        acc_ref[...] = alpha * acc_ref[...] + weighted_values(p.astype(BF16))
        m_ref[...] = m_new

    scores = [_dot(q, k_ref[0].reshape(D_DIFF, page).astype(BF16)) for k_ref in k_refs]
    scores[-1] = scores[-1] + jnp.where(j == n_groups - 1, band_ref[0], 0.0)

    rows = q.shape[0]
    row_head = (lax.broadcasted_iota(I32, (rows, page * H_DIFF), 0) % (H_DIFF * n_q)) // n_q
    own_rows = (lax.broadcasted_iota(I32, (rows, page * H_DIFF), 1) % H_DIFF) == row_head

    def cached_values(p):
        o = None
        for u, v_ref in enumerate(v_refs):
            spread = _dot(p[:, u * page:(u + 1) * page], expand_ref[...])
            part = _dot(jnp.where(own_rows, spread, 0.0).astype(BF16), v_ref[0].astype(BF16))
            o = part if o is None else o + part
        return o

    update(jnp.concatenate(scores, axis=1), cached_values)

    @pl.when(j == n_groups - 1)
    def _():
        def new_values(p):
            full = _dot(p, vn_ref[0])
            head = (lax.broadcasted_iota(I32, (rows, 2 * DH), 0) % (H_DIFF * n_q)) // n_q
            return sum(jnp.where(head == h, full[:, h * 2 * DH:(h + 1) * 2 * DH], 0.0) for h in range(H_DIFF))

        update(_dot_nt(q, kn_ref[0]) + band_ref[1], new_values)
        o = acc_ref[...] / l_ref[...]
        half = rows // 2
        o = o[:half] - lam_ref[0] * o[half:]
        o = _rms(o, gd_ref[...]) * (1.0 - LAM_INIT)
        o_ref[0] = jnp.concatenate([o[h * n_q:(h + 1) * n_q] for h in range(H_DIFF)], axis=1).astype(BF16)


PAGE_GROUP = 8


def sample_attention(page_table, lam, qst, cache_kt, cache_v, kn, vn, band, gd, n_q):
    b, n_pages = page_table.shape
    page = cache_kt.shape[-1]
    rows = n_q * DEC_ROWS
    expand = jnp.repeat(jnp.eye(page, dtype=BF16), H_DIFF, axis=1)
    group = math.gcd(PAGE_GROUP, n_pages)
    n_groups = n_pages // group

    def page_of(u):
        return lambda i, j, pt: (pt[i, j * group + u],)

    k_specs = [pl.BlockSpec((1,) + cache_kt.shape[1:], lambda i, j, pt, f=page_of(u): f(i, j, pt) + (0, 0, 0, 0))
               for u in range(group)]
    v_specs = [pl.BlockSpec((1,) + cache_v.shape[1:], lambda i, j, pt, f=page_of(u): f(i, j, pt) + (0, 0))
               for u in range(group)]
    new_spec = pl.BlockSpec((1, page, D_DIFF), lambda i, j, pt: (i, 0, 0))
    grid_spec = pltpu.PrefetchScalarGridSpec(
        num_scalar_prefetch=1,
        grid=(b, n_groups),
        in_specs=[pl.BlockSpec(memory_space=pltpu.SMEM),
                  pl.BlockSpec((1, rows, D_DIFF), lambda i, j, pt: (i, 0, 0))]
                 + k_specs + v_specs
                 + [new_spec, new_spec,
                    pl.BlockSpec((2, rows, page), lambda i, j, pt: (0, 0, 0)),
                    pl.BlockSpec(expand.shape, lambda i, j, pt: (0, 0)),
                    pl.BlockSpec((1, 2 * DH), lambda i, j, pt: (0, 0))],
        out_specs=pl.BlockSpec((1, n_q, D_DIFF), lambda i, j, pt: (i, 0, 0)),
        scratch_shapes=[pltpu.VMEM((rows, 1), F32), pltpu.VMEM((rows, 1), F32), pltpu.VMEM((rows, 2 * DH), F32)],
    )
    return pl.pallas_call(
        functools.partial(_dec_attn_kernel, n_groups=n_groups, n_q=n_q, group=group),
        grid_spec=grid_spec,
        out_shape=jax.ShapeDtypeStruct((b, n_q, D_DIFF), BF16),
        compiler_params=_params("parallel", "arbitrary"),
        name="sample_attention",
    )(page_table, lam, qst, *([cache_kt] * group), *([cache_v] * group), kn, vn, band, expand, gd)


def _mid_kernel(x_ref, ys_ref, yd_ref, wa_ref, wb_ref, g_ref, wq_ref, h_ref, m_ref, qp_ref):
    h = x_ref[...] + _dot(ys_ref[...], wa_ref[...]) + _dot(yd_ref[...], wb_ref[...])
    h_ref[...] = h
    m = _rms(h, g_ref[...]).astype(BF16)
    mf = m.astype(F32)
    for c in range(D_MODEL // LANES):
        m_ref[:, c, :] = mf[:, c * LANES:(c + 1) * LANES]
    qp_ref[...] = _dot(m, wq_ref[...]).astype(BF16)


def mid(x, ys, yd, wa, wb, g, wq):
    t = x.shape[0]
    tm = min(ROW_TILE, t)
    row = lambda n: pl.BlockSpec((tm, n), lambda i: (i, 0))
    nq = wq.shape[1]
    return pl.pallas_call(
        _mid_kernel,
        grid=(t // tm,),
        in_specs=[row(D_MODEL), row(D_SSM), row(D_DIFF), _const_spec(wa.shape), _const_spec(wb.shape),
                  _const_spec((1, D_MODEL)), _const_spec(wq.shape)],
        out_specs=(row(D_MODEL), pl.BlockSpec((tm, D_MODEL // LANES, LANES), lambda i: (i, 0, 0)), row(nq)),
        out_shape=(jax.ShapeDtypeStruct((t, D_MODEL), F32), jax.ShapeDtypeStruct((t, D_MODEL // LANES, LANES), F32),
                   jax.ShapeDtypeStruct((t, nq), BF16)),
        compiler_params=_params("parallel"),
        name="mid",
    )(x, ys, yd, wa, wb, g, wq)


def _top16(vals):
    n = vals.shape[0]
    from_end = (n - 1 - lax.broadcasted_iota(I32, vals.shape, 0)).astype(F32)
    out_v, out_i = [], []
    cur = vals
    for _ in range(PEER_TOPK):
        mx = jnp.max(cur, axis=0, keepdims=True)
        at = jnp.max(jnp.where(cur == mx, from_end, -1.0), axis=0, keepdims=True)
        out_v.append(mx)
        out_i.append(at)
        cur = jnp.where(from_end == at, -jnp.inf, cur)
    return jnp.concatenate(out_v, axis=0), n - 1 - jnp.concatenate(out_i, axis=0).astype(I32)


HI_MASK = -65536
LOW_HALF_SHIFT = 16


def _topk_kernel(qp_ref, sk_ref, row_ref, shift_ref, shift_t_ref, gate_ref):
    tb = qp_ref.shape[0]
    half = D_KEY // 2
    idx_rows, gate_rows = [], []
    for h in range(PEER_HEADS):
        s, ix = [], []
        for c in range(2):
            qh = qp_ref[:, (2 * h + c) * half:(2 * h + c + 1) * half]
            sc = _dot_nt(sk_ref[c], qh)
            v, i = _top16(sc)
            s.append(v)
            ix.append(i)
        pieces = [(a, PEER_TOPK if a == 0 else SUBLANES) for a in range(SUBLANES)]
        cand = jnp.concatenate([s[0][a:a + 1, :] + s[1][:nb] for a, nb in pieces] + [s[0][SUBLANES:] + s[1][0:1]],
                               axis=0)
        cidx = jnp.concatenate([ix[0][a:a + 1, :] * N_KEYS + ix[1][:nb] for a, nb in pieces]
                               + [ix[0][SUBLANES:] * N_KEYS + ix[1][0:1]], axis=0)
        sc, pos = _top16(cand)
        prow = lax.broadcasted_iota(I32, cand.shape, 0)
        cidx_f = cidx.astype(F32)
        idx = jnp.concatenate(
            [jnp.max(jnp.where(prow == pos[k:k + 1, :], cidx_f, -1.0), axis=0, keepdims=True)
             for k in range(PEER_TOPK)], axis=0).astype(I32)
        e = jnp.exp(sc - sc[0:1, :])
        gate_rows.append(e / jnp.sum(e, axis=0, keepdims=True))
        idx_rows.append(idx)
    idx_t = jnp.concatenate(idx_rows, axis=0)
    shift_t = ((idx_t & 1) ^ 1) * LOW_HALF_SHIFT
    shift_t_ref[0] = shift_t.astype(F32)
    idx = idx_t.T
    gate_ref[...] = jnp.concatenate(gate_rows, axis=0).T
    row_ref[...] = (idx >> 1) * SUBLANES
    shift_ref[...] = shift_t.T


def peer_topk(qp, sk):
    t = qp.shape[0]
    tb = min(PEER_TILE, t)
    out = pl.BlockSpec((tb, SLOTS), lambda i: (i, 0))
    return pl.pallas_call(
        _topk_kernel,
        grid=(t // tb,),
        in_specs=[pl.BlockSpec((tb, qp.shape[1]), lambda i: (i, 0)), _const_spec(sk.shape)],
        out_specs=(out, out, pl.BlockSpec((1, SLOTS, tb), lambda i: (i, 0, 0)), out),
        out_shape=(jax.ShapeDtypeStruct((t, SLOTS), I32), jax.ShapeDtypeStruct((t, SLOTS), I32),
                   jax.ShapeDtypeStruct((t // tb, SLOTS, tb), F32), jax.ShapeDtypeStruct((t, SLOTS), F32)),
        compiler_params=_params("parallel"),
        name="peer_topk",
    )(qp, sk)


def _pack_kernel(w_ref, o_ref):
    n = o_ref.shape[0]
    even = w_ref[pl.ds(0, n, stride=2), :].astype(BF16).astype(F32)
    odd = w_ref[pl.ds(1, n, stride=2), :].astype(BF16).astype(F32)
    lo = (lax.bitcast_convert_type(even, I32) >> 16) & 0xFFFF
    o_ref[...] = lo | (lax.bitcast_convert_type(odd, I32) & HI_MASK)


def pack_table(w):
    n, d = w.shape
    rows = 2048
    packed = pl.pallas_call(
        _pack_kernel,
        grid=(n // (2 * rows), d // LANES),
        in_specs=[pl.BlockSpec((2 * rows, LANES), lambda i, c: (i, c))],
        out_specs=pl.BlockSpec((rows, LANES), lambda i, c: (i, c)),
        out_shape=jax.ShapeDtypeStruct((n // 2, d), I32),
        compiler_params=_params("parallel", "parallel"),
        name="pack_table",
    )(w)
    return packed.reshape(n // 2 * (d // LANES), LANES)


def _expert_row(tab_ref, row, shift):
    word = tab_ref[pl.ds(pl.multiple_of(row, SUBLANES), SUBLANES), :]
    return lax.bitcast_convert_type((word << shift) & HI_MASK, F32)


UNIT = SUBLANES


def _butterfly(parts, add, roll, select):
    def node(first, count, offset):
        if count == 1:
            return parts[first]
        run = SUBLANES // count
        half = count // 2
        o1 = offset % (2 * run)
        a = node(first, half, o1)
        b = node(first + half, half, (o1 + run) % (2 * run))
        takes_a = lambda p: ((p - o1) % (2 * run)) >= run
        return add(select(takes_a, a, b), roll(select(takes_a, b, a), run))

    return node(0, SUBLANES, 0)


def _butterfly_order():
    tiles = [[{p}] * SUBLANES for p in range(SUBLANES)]
    out = _butterfly(
        tiles,
        add=lambda a, b: [x | y for x, y in zip(a, b)],
        roll=lambda a, k: [a[(i - k) % SUBLANES] for i in range(SUBLANES)],
        select=lambda takes_a, a, b: [a[i] if takes_a(i) else b[i] for i in range(SUBLANES)])
    order = [0] * SUBLANES
    for sublane, owners in enumerate(out):
        (p,) = owners
        order[p] = sublane
    return order


BUTTERFLY_ORDER = _butterfly_order()


def _rows_to_sublanes(parts):
    sub = lax.broadcasted_iota(I32, (SUBLANES, LANES), 0)
    placed = [None] * SUBLANES
    for i, part in enumerate(parts):
        placed[BUTTERFLY_ORDER.index(i)] = part
    return _butterfly(placed, add=lambda a, b: a + b, roll=lambda a, k: pltpu.roll(a, k, 0),
                      select=lambda takes_a, a, b: jnp.where(takes_a(sub), a, b))


def _splat(ref, j, t):
    return jnp.broadcast_to(ref[j:j + 1, t * LANES:(t + 1) * LANES], (SUBLANES, LANES))


def _peer_u_kernel(row_ref, m_ref, gate_ref, shiftv_ref, shift_t_ref, spread_ref, tab_ref, wt_ref,
                   part_ref, h_ref, pad_ref, shift_splat_ref):
    tb = m_ref.shape[0]
    step = pl.program_id(0)

    @pl.when(step == 0)
    def _():
        pad_ref[...] = jnp.zeros(pad_ref.shape, F32)

    picks = spread_ref[step % spread_ref.shape[0]]
    shift_splat_ref[...] = _dot(shift_t_ref[0].astype(BF16), picks).astype(I32)

    def token(t):
        x = m_ref[t]
        for u in range(SLOTS // UNIT):
            parts = [_expert_row(tab_ref, row_ref[t, j], _splat(shift_splat_ref, j, t)) * x
                     for j in range(u * UNIT, (u + 1) * UNIT)]
            part_ref[t, u * UNIT:(u + 1) * UNIT, :] = _rows_to_sublanes(parts)

    for t in range(tb):
        token(t)
    for t in range(tb):
        h_ref[t:t + 1, :] = jnp.sum(part_ref[t].T, axis=0, keepdims=True)
    h = h_ref[...]
    act = 0.5 * h * (1.0 + lax.erf(h * (2.0 ** -0.5)))
    pad_ref[0:tb, :] = (gate_ref[...] * act).astype(BF16).astype(F32)
    pad_ref[tb:2 * tb, :] = shiftv_ref[...].astype(F32)
    wt_ref[0] = pad_ref[...].T


def _token_picks(n_tokens, tb):
    eye = jnp.eye(n_tokens, dtype=BF16).reshape(n_tokens, n_tokens // tb, tb)
    return jnp.repeat(jnp.moveaxis(eye, 1, 0), LANES, axis=2)


def peer_u(row, shift, shift_t, m3, gate, tab):
    t = m3.shape[0]
    tb = min(EXPERT_TILE, t)
    top_tb = shift_t.shape[2]
    per_top = top_tb // tb
    picks = _token_picks(top_tb, tb)
    smem = pl.BlockSpec((tb, SLOTS), lambda i: (i, 0), memory_space=pltpu.SMEM)
    vmem = pl.BlockSpec((tb, SLOTS), lambda i: (i, 0))
    return pl.pallas_call(
        _peer_u_kernel,
        grid=(t // tb,),
        in_specs=[smem, pl.BlockSpec((tb, SUBLANES, LANES), lambda i: (i, 0, 0)), vmem, vmem,
                  pl.BlockSpec((1, SLOTS, top_tb), lambda i: (i // per_top, 0, 0)), _const_spec(picks.shape),
                  pl.BlockSpec(tab.shape, lambda i: (0, 0), pipeline_mode=pl.Buffered(1))],
        out_specs=pl.BlockSpec((1, SLOTS, LANES), lambda i: (i, 0, 0)),
        out_shape=jax.ShapeDtypeStruct((t // tb, SLOTS, LANES), F32),
        scratch_shapes=[pltpu.VMEM((tb, SLOTS, LANES), F32), pltpu.VMEM((tb, SLOTS), F32),
                        pltpu.VMEM((LANES, SLOTS), F32), pltpu.VMEM((SLOTS, tb * LANES), I32)],
        compiler_params=_params("arbitrary"),
        name="peer_u",
    )(row, m3, gate, shift, shift_t, picks, tab)


N_ACC = 8


REGION_TOKENS = 4


def _token_regions(n, body):
    always = pl.program_id(0) >= 0

    def region(first):
        for t in range(first, min(first + REGION_TOKENS, n)):
            body(t)

    for first in range(0, n, REGION_TOKENS):
        pl.when(always)(functools.partial(region, first))


def _peer_v_kernel(row_ref, wt_ref, spread_ref, tab_ref, o_ref, code_ref):
    tb = o_ref.shape[0]
    spread = _dot(wt_ref[0][:, :2 * tb].astype(BF16), spread_ref[...])
    code_ref[...] = (lax.bitcast_convert_type(spread[:, :tb * LANES], I32)
                     | spread[:, tb * LANES:].astype(I32))

    def token(t):
        acc = [None] * N_ACC
        for j in range(SLOTS):
            code = _splat(code_ref, j, t)
            weight = lax.bitcast_convert_type(code & HI_MASK, F32)
            term = weight * _expert_row(tab_ref, row_ref[t, j], code & LOW_HALF_SHIFT)
            acc[j % N_ACC] = term if acc[j % N_ACC] is None else acc[j % N_ACC] + term
        while len(acc) > 1:
            acc = [a + b for a, b in zip(acc[0::2], acc[1::2])]
        o_ref[t] = acc[0]

    _token_regions(tb, token)


def peer_v(row, wt, tab):
    t = row.shape[0]
    tb = min(EXPERT_TILE, t)
    smem = pl.BlockSpec((tb, SLOTS), lambda i: (i, 0), memory_space=pltpu.SMEM)
    spread = jnp.repeat(jnp.eye(2 * tb, dtype=BF16), LANES, axis=1)
    return pl.pallas_call(
        _peer_v_kernel,
        grid=(t // tb,),
        in_specs=[smem, pl.BlockSpec((1, SLOTS, LANES), lambda i: (i, 0, 0)), _const_spec(spread.shape),
                  pl.BlockSpec(tab.shape, lambda i: (0, 0), pipeline_mode=pl.Buffered(1))],
        out_specs=pl.BlockSpec((tb, SUBLANES, LANES), lambda i: (i, 0, 0)),
        out_shape=jax.ShapeDtypeStruct((t, SUBLANES, LANES), F32),
        scratch_shapes=[pltpu.VMEM((SLOTS, tb * LANES), I32)],
        compiler_params=_params("arbitrary"),
        name="peer_v",
    )(row, wt, spread, tab)


def _ple_kernel(h_ref, peer_ref, p_ref, wp_ref, wg_ref, gp_ref, gf_ref, y_ref):
    peer = jnp.concatenate([peer_ref[:, c, :] for c in range(D_MODEL // LANES)], axis=1)
    h = h_ref[...] + peer
    gate = jax.nn.sigmoid(_dot(_rms(h, gp_ref[...]).astype(BF16), wg_ref[...]))
    h = h + _dot(p_ref[...].astype(BF16), wp_ref[...]) * gate
    y_ref[...] = _rms(h, gf_ref[...])


def ple(h, peer_out, p, wp, wg, gp, gf):
    t = h.shape[0]
    tm = min(ROW_TILE, t)
    row = lambda n: pl.BlockSpec((tm, n), lambda i: (i, 0))
    return pl.pallas_call(
        _ple_kernel,
        grid=(t // tm,),
        in_specs=[row(D_MODEL), pl.BlockSpec((tm, D_MODEL // LANES, LANES), lambda i: (i, 0, 0)), row(PLE_DIM),
                  _const_spec(wp.shape), _const_spec(wg.shape),
                  _const_spec((1, D_MODEL)), _const_spec((1, D_MODEL))],
        out_specs=row(D_MODEL),
        out_shape=jax.ShapeDtypeStruct((t, D_MODEL), F32),
        compiler_params=_params("parallel"),
        name="ple",
    )(h, peer_out, p, wp, wg, gp, gf)


def _t5_bucket(rel):
    n = jnp.maximum(rel, 0)
    max_exact = N_BUCKETS // 2
    nf = jnp.maximum(n, 1).astype(F32)
    large = max_exact + (jnp.log(nf / max_exact) / math.log(REL_MAX_DIST / max_exact)
                         * (N_BUCKETS - max_exact)).astype(I32)
    large = jnp.minimum(large, N_BUCKETS - 1)
    return jnp.where(n < max_exact, n, large)


def _band_bias(rel_bias, rel):
    far = rel_bias[N_BUCKETS - 1]
    bucket = _t5_bucket(rel)[..., None]
    b = jnp.zeros(rel.shape + (H_DIFF,), F32)
    for n in range(N_BUCKETS - 1):
        b = jnp.where(bucket == n, rel_bias[n] - far, b)
    b = jnp.where((rel >= 0)[..., None], b, NEG_INF)
    return jnp.moveaxis(b, -1, 0).astype(F32)


def _pad_rows(a, rows):
    return jnp.pad(a, ((0, 0), (0, rows - a.shape[1]), (0, 0)))


def _layer(x, p_l, conv_buf, ssm0, lam, w, attention):
    b, l, _ = x.shape
    t = b * l
    x2 = x.reshape(t, D_MODEL)
    z, xbc, q2, k, v, kb, vb, dt = in_proj(x2, w["g_mix"], w["wz"], w["wx"], w["wq"], w["wk"], w["wv"], w["wdt"], l)
    if k.ndim == 3:
        k = jnp.transpose(k.reshape(b, H_DIFF, 2, DH, l), (0, 4, 1, 2, 3))
    else:
        k = k.reshape(b, l, H_DIFF, 2, DH)

    lp = -(-l // SSD_CHUNK) * SSD_CHUNK
    xbc3 = xbc.reshape(b, l, CONV_DIM)
    cbuf = jnp.pad(conv_buf, ((0, 0), (SUBLANES - (CONV_W - 1), 0), (0, 0)))
    y_ssm, ssm_final = ssd(_pad_rows(xbc3, lp), _pad_rows(z.reshape(b, l, D_SSM), lp),
                           _pad_rows(dt.reshape(b, l, LANES), lp), cbuf, ssm0,
                           w["conv_w"], w["conv_b"], w["dt_bias"], w["a_log"], w["d_skip"], w["g_ssm_norm"], l)
    y_ssm = y_ssm[:, :l].reshape(t, D_SSM)
    tail = CONV_W - 1
    conv_new = xbc3[:, l - tail:] if l >= tail else jnp.concatenate([conv_buf, xbc3], axis=1)[:, -tail:]

    y_diff = attention(q2, kb, vb)

    h1, m, qp = mid(x2, y_ssm, y_diff, w["wo_a"], w["wo_b"], w["g_ffn"], w["w_query"])
    row, shift, shift_t, gate = peer_topk(qp, w["sub_keys"])
    wt = peer_u(row, shift, shift_t, m, gate, w["tab_u"])
    peer_out = peer_v(row, wt, w["tab_v"])
    y = ple(h1, peer_out, p_l.reshape(t, PLE_DIM), w["w_ple"], w["w_ple_gate"], w["g_ple"], w["g_final"])
    return (y.reshape(b, l, D_MODEL), k, v.reshape(b, l, H_DIFF, 2 * DH),
            ssm_final, conv_new)


def kernel(x_prompt, x_sample, cache_k, cache_v, state_ssm, state_conv, page_table, p_prompt, p_sample, g_mix, w_in, conv_w, conv_b, dt_bias, a_log, d_skip, g_ssm_norm, lam_q1, lam_k1, lam_q2, lam_k2, g_diff_norm, rel_bias, w_out, g_ffn, w_query, sub_keys, expert_u, expert_v, w_ple, w_ple_gate, g_ple, g_final):
    assert w_in.shape[0] == 1, "single-layer problem"
    bsz, seq, _ = x_prompt.shape
    dec_b, dec_l, _ = x_sample.shape
    n_pages = page_table.shape[1]
    page = cache_k.shape[2]
    past = n_pages * page

    wi = w_in[0].astype(BF16)
    o_x = D_SSM
    o_dt = o_x + CONV_DIM
    o_q = o_dt + H_SSM
    o_k = o_q + D_DIFF
    o_v = o_k + D_DIFF
    lanes_of = lambda a: jnp.pad(a.astype(F32)[None, :], ((0, 0), (0, LANES - a.shape[0])))
    w = dict(
        g_mix=g_mix[0][None], wz=wi[:, :o_x], wx=wi[:, o_x:o_dt],
        wdt=jnp.pad(wi[:, o_dt:o_q], ((0, 0), (0, LANES - H_SSM))),
        wq=wi[:, o_q:o_k], wk=wi[:, o_k:o_v], wv=wi[:, o_v:],
        conv_w=conv_w[0], conv_b=conv_b[0][None], dt_bias=lanes_of(dt_bias[0]), a_log=lanes_of(a_log[0]),
        d_skip=lanes_of(d_skip[0]), g_ssm_norm=g_ssm_norm[0][None],
        wo_a=w_out[0][:D_SSM].astype(BF16), wo_b=w_out[0][D_SSM:].astype(BF16), g_ffn=g_ffn[0][None],
        w_query=w_query[0].astype(BF16), sub_keys=sub_keys[0].astype(BF16),
        tab_u=pack_table(expert_u[0]), tab_v=pack_table(expert_v[0]),
        w_ple=w_ple[0].astype(BF16), w_ple_gate=w_ple_gate[0].astype(BF16), g_ple=g_ple[0][None],
        g_final=g_final[None],
    )
    gd = g_diff_norm[0][None]
    lam = (jnp.exp(jnp.sum(lam_q1[0] * lam_k1[0])) - jnp.exp(jnp.sum(lam_q2[0] * lam_k2[0])) + LAM_INIT).reshape(1)

    tq = min(ATTN_TILE, seq)
    ii = jnp.arange(tq, dtype=I32)
    assert tq >= REL_MAX_DIST and page >= REL_MAX_DIST, "bias is only applied next to the diagonal"
    rel = jnp.stack([ii[None, :] - ii[:, None], tq + ii[None, :] - ii[:, None]])
    band_p = _band_bias(rel_bias, rel)

    def attn_prompt(q2, kb, vb):
        return prompt_attention(lam, q2, kb, vb, band_p, gd, bsz, seq, tq).reshape(bsz * seq, D_DIFF)

    yp, kp, vp, sp, cp = _layer(x_prompt, p_prompt[0], jnp.zeros((bsz, CONV_W - 1, CONV_DIM), F32),
                                jnp.zeros((bsz, H_SSM, P_SSM, N_STATE), F32), lam, w, attn_prompt)

    rows = dec_l * DEC_ROWS
    r = jnp.arange(rows, dtype=I32)
    qpos = past + r % dec_l
    head_of = (r % (H_DIFF * dec_l)) // dec_l
    map_of = r // (H_DIFF * dec_l)
    kk = jnp.arange(page, dtype=I32)
    rel_last = qpos[:, None] - (past - page + kk)[None, :]
    rel_new = jnp.where(kk[None, :] < dec_l, qpos[:, None] - (past + kk)[None, :], -1)
    band_all = _band_bias(rel_bias, jnp.stack([rel_last, rel_new]))
    band_s = jnp.take_along_axis(band_all, head_of[None, None, :, None], axis=0)[0]
    lane_slot = jnp.arange(D_DIFF, dtype=I32) // DH
    q_mask = (lane_slot[None, :] == (head_of * 2 + map_of)[:, None])
    cache_v2 = cache_v[0].reshape(-1, page * H_DIFF, 2 * DH)

    def attn_sample(q2, kb, vb):
        q = (q2[0] + q2[1]).reshape(dec_b, dec_l, D_DIFF)
        qst = jnp.where(q_mask[None], jnp.tile(q, (1, DEC_ROWS, 1)), jnp.zeros((), BF16))
        kn = _pad_rows(kb.reshape(dec_b, dec_l, D_DIFF), page)
        vn = _pad_rows(vb.reshape(dec_b, dec_l, D_DIFF), page)
        o = sample_attention(page_table, lam, qst, jnp.transpose(cache_k[0], (0, 2, 3, 4, 1)), cache_v2,
                             kn, vn, band_s, gd, dec_l)
        return o.reshape(dec_b * dec_l, D_DIFF)

    ys, ks, vs, ss, cs = _layer(x_sample, p_sample[0], state_conv[0], state_ssm[0], lam, w, attn_sample)

    return (yp, ys, kp[None], vp[None], sp[None], cp[None], ks[None], vs[None], ss[None], cs[None])
```

```python
import functools
import math

import jax
import jax.numpy as jnp
from jax import lax
from jax.experimental import pallas as pl
from jax.experimental.pallas import tpu as pltpu

F32 = jnp.float32
BF16 = jnp.bfloat16
I32 = jnp.int32

D_MODEL = 1024
D_SSM = 512
D_DIFF = 512
P_SSM = 64
H_SSM = 8
N_STATE = 128
SSM_GROUPS = 2
HEADS_PER_GROUP = H_SSM // SSM_GROUPS
CONV_W = 4
CONV_DIM = D_SSM + 2 * SSM_GROUPS * N_STATE
DH = 64
H_DIFF = 4
N_BUCKETS = 32
REL_MAX_DIST = 128
N_KEYS = 128
PEER_HEADS = 8
PEER_TOPK = 16
D_KEY = 256
PLE_DIM = 256
NEG_INF = -1e30
RMS_EPS = 1e-6
LAM_INIT = 0.8 - 0.6 * math.exp(-0.3 * 0)

LANES = 128
SUBLANES = 8
VMEM_LIMIT = 56 * 1024 * 1024

SSD_CHUNK = 128
ROW_TILE = 256
ATTN_TILE = 1024
PEER_TILE = 128
EXPERT_TILE = 32
SLOTS = PEER_HEADS * PEER_TOPK


def _params(*sem):
    return pltpu.CompilerParams(dimension_semantics=sem, vmem_limit_bytes=VMEM_LIMIT)


def _const_spec(shape):
    nd = len(shape)
    return pl.BlockSpec(shape, lambda *_: (0,) * nd)


def _rms(x, g):
    return x * lax.rsqrt(jnp.mean(x * x, axis=-1, keepdims=True) + RMS_EPS) * g


def _dot(a, b):
    return jnp.dot(a, b, preferred_element_type=F32)


def _dot_nt(a, b):
    return lax.dot_general(a, b, (((1,), (1,)), ((), ())), preferred_element_type=F32)


def _dot_tn(a, b):
    return lax.dot_general(a, b, (((0,), (0,)), ((), ())), preferred_element_type=F32)


def _in_proj_kernel(x_ref, g_ref, wz_ref, wx_ref, wq_ref, wk_ref, wv_ref, wdt_ref,
                    z_ref, xbc_ref, q2_ref, k_ref, v_ref, kb_ref, vb_ref, dt_ref, *, transposed_k):
    a = _rms(x_ref[...], g_ref[...]).astype(BF16)
    z_ref[...] = _dot(a, wz_ref[...])
    xbc_ref[...] = _dot(a, wx_ref[...])
    dt_ref[...] = _dot(a, wdt_ref[...])
    k = _dot(a, wk_ref[...])
    v = _dot(a, wv_ref[...])
    if transposed_k:
        k_ref[0] = k.T
    else:
        k_ref[...] = k
    for h in range(H_DIFF):
        v_ref[:, h, :] = v[:, h * 2 * DH:(h + 1) * 2 * DH]
    kb_ref[...] = k.astype(BF16)
    vb_ref[...] = v.astype(BF16)
    q = _dot(a, wq_ref[...]).astype(BF16) * (DH ** -0.5)
    lane = lax.broadcasted_iota(I32, q.shape, 1)
    first_map = (lane % (2 * DH)) < DH
    zero = jnp.zeros_like(q)
    q2_ref[0] = jnp.where(first_map, q, zero)
    q2_ref[1] = jnp.where(first_map, zero, q)


def in_proj(x, g, wz, wx, wq, wk, wv, wdt, seq):
    t = x.shape[0]
    tm = min(ROW_TILE, t)
    row = lambda n: pl.BlockSpec((tm, n), lambda i: (i, 0))
    transposed_k = seq % tm == 0
    per_seq = seq // tm if transposed_k else 1
    if transposed_k:
        k_shape = jax.ShapeDtypeStruct((t // seq, D_DIFF, seq), F32)
        k_spec = pl.BlockSpec((1, D_DIFF, tm), lambda i: (i // per_seq, 0, i % per_seq))
    else:
        k_shape, k_spec = jax.ShapeDtypeStruct((t, D_DIFF), F32), row(D_DIFF)
    out_shape = (
        jax.ShapeDtypeStruct((t, D_SSM), F32),
        jax.ShapeDtypeStruct((t, CONV_DIM), F32),
        jax.ShapeDtypeStruct((2, t, D_DIFF), BF16),
        k_shape,
        jax.ShapeDtypeStruct((t, H_DIFF, 2 * DH), F32),
        jax.ShapeDtypeStruct((t, D_DIFF), BF16),
        jax.ShapeDtypeStruct((t, D_DIFF), BF16),
        jax.ShapeDtypeStruct((t, LANES), F32),
    )
    return pl.pallas_call(
        functools.partial(_in_proj_kernel, transposed_k=transposed_k),
        grid=(t // tm,),
        in_specs=[row(D_MODEL), _const_spec((1, D_MODEL)), _const_spec(wz.shape), _const_spec(wx.shape),
                  _const_spec(wq.shape), _const_spec(wk.shape), _const_spec(wv.shape), _const_spec(wdt.shape)],
        out_specs=(row(D_SSM), row(CONV_DIM), pl.BlockSpec((2, tm, D_DIFF), lambda i: (0, i, 0)),
                   k_spec, pl.BlockSpec((tm, H_DIFF, 2 * DH), lambda i: (i, 0, 0)), row(D_DIFF), row(D_DIFF),
                   row(LANES)),
        out_shape=out_shape,
        compiler_params=_params("parallel"),
        name="in_proj",
    )(x, g, wz, wx, wq, wk, wv, wdt)


def _ssd_kernel(xbc_ref, z_ref, dt_ref, cbuf_ref, s0_ref, cw_ref, cb_ref, dtb_ref, alog_ref, dskip_ref, gn_ref,
                y_ref, sfin_ref, win_ref, state_ref, *, seq_len):
    c = pl.program_id(1)
    lc = SSD_CHUNK

    @pl.when(c == 0)
    def _():
        win_ref[0:SUBLANES, :] = cbuf_ref[0]
        state_ref[...] = s0_ref[0]

    win_ref[SUBLANES:SUBLANES + lc, :] = xbc_ref[0]
    base = SUBLANES - (CONV_W - 1)
    conv = cb_ref[...]
    for j in range(CONV_W):
        conv = conv + cw_ref[j:j + 1, :] * win_ref[base + j:base + j + lc, :]
    win_ref[0:SUBLANES, :] = win_ref[lc:lc + SUBLANES, :]
    xc = conv * jax.nn.sigmoid(conv)

    pos = c * lc + lax.broadcasted_iota(I32, (lc, LANES), 0)
    dt = jax.nn.softplus(dt_ref[0] + dtb_ref[...])
    dt = jnp.where(pos < seq_len, dt, 0.0)
    adt = -jnp.exp(alog_ref[...]) * dt
    row = lax.broadcasted_iota(I32, (lc, lc), 0)
    col = lax.broadcasted_iota(I32, (lc, lc), 1)
    causal = row >= col
    tri = causal.astype(F32)
    cs = jnp.dot(tri, adt, preferred_element_type=F32, precision=lax.Precision.HIGHEST)
    cs_t = cs.T
    cs_last = cs[lc - 1:lc, :]

    ys = []
    for g in range(SSM_GROUPS):
        b0 = D_SSM + g * N_STATE
        c0 = D_SSM + SSM_GROUPS * N_STATE + g * N_STATE
        bg = xc[:, b0:b0 + N_STATE]
        cg = xc[:, c0:c0 + N_STATE].astype(BF16)
        cb = _dot_nt(cg, bg.astype(BF16))
        for hh in range(HEADS_PER_GROUP):
            h = g * HEADS_PER_GROUP + hh
            xs = xc[:, h * P_SSM:(h + 1) * P_SSM]
            cs_h = cs[:, h:h + 1]
            decay = jnp.where(causal, jnp.exp(jnp.where(causal, cs_h - cs_t[h:h + 1, :], 0.0)), 0.0)
            xdt = xs * dt[:, h:h + 1]
            y = _dot((cb * decay).astype(BF16), xdt.astype(BF16))
            st = state_ref[h]
            y = y + jnp.exp(cs_h) * _dot_nt(cg, st.astype(BF16))
            y = y + dskip_ref[:, h:h + 1] * xs
            ys.append(y)
            to_end = jnp.exp(cs_last[:, h:h + 1] - cs_h)
            new = _dot_tn((xdt * to_end).astype(BF16), bg.astype(BF16))
            state_ref[h] = jnp.exp(cs_last[:, h:h + 1]) * st + new
    y = jnp.concatenate(ys, axis=1)
    zz = z_ref[0]
    y = y * (zz * jax.nn.sigmoid(zz))
    gw = D_SSM // SSM_GROUPS
    outs = []
    for g in range(SSM_GROUPS):
        yg = y[:, g * gw:(g + 1) * gw]
        outs.append(yg * lax.rsqrt(jnp.mean(yg * yg, axis=-1, keepdims=True) + RMS_EPS))
    y_ref[0] = (jnp.concatenate(outs, axis=1) * gn_ref[...]).astype(BF16)

    @pl.when(c == pl.num_programs(1) - 1)
    def _():
        sfin_ref[0] = state_ref[...]


def ssd(xbc, z, dt, cbuf, s0, cw, cb, dtb, alog, dskip, gn, seq_len):
    b, lp, _ = xbc.shape
    lc = SSD_CHUNK
    seq = lambda n: pl.BlockSpec((1, lc, n), lambda i, c: (i, c, 0))
    return pl.pallas_call(
        functools.partial(_ssd_kernel, seq_len=seq_len),
        grid=(b, lp // lc),
        in_specs=[seq(CONV_DIM), seq(D_SSM), seq(LANES),
                  pl.BlockSpec((1, SUBLANES, CONV_DIM), lambda i, c: (i, 0, 0)),
                  pl.BlockSpec((1, H_SSM, P_SSM, N_STATE), lambda i, c: (i, 0, 0, 0)),
                  _const_spec(cw.shape), _const_spec(cb.shape), _const_spec(dtb.shape), _const_spec(alog.shape),
                  _const_spec(dskip.shape), _const_spec(gn.shape)],
        out_specs=(seq(D_SSM), pl.BlockSpec((1, H_SSM, P_SSM, N_STATE), lambda i, c: (i, 0, 0, 0))),
        out_shape=(jax.ShapeDtypeStruct((b, lp, D_SSM), BF16),
                   jax.ShapeDtypeStruct((b, H_SSM, P_SSM, N_STATE), F32)),
        scratch_shapes=[pltpu.VMEM((SUBLANES + lc, CONV_DIM), F32), pltpu.VMEM((H_SSM, P_SSM, N_STATE), F32)],
        compiler_params=_params("parallel", "arbitrary"),
        name="ssd",
    )(xbc, z, dt, cbuf, s0, cw, cb, dtb, alog, dskip, gn)


def _softmax_tile(q, k, v, bias, m_ref, l_ref, acc_ref):
    s = _dot_nt(q, k)
    if bias is not None:
        s = s + bias
    m_old = m_ref[...]
    m_new = jnp.maximum(m_old, jnp.max(s, axis=-1, keepdims=True))
    alpha = jnp.exp(m_old - m_new)
    p = jnp.exp(s - m_new)
    l_ref[...] = alpha * l_ref[...] + jnp.sum(p, axis=-1, keepdims=True)
    acc_ref[...] = alpha * acc_ref[...] + _dot(p.astype(BF16), v)
    m_ref[...] = m_new


def _softmax_init(m_ref, l_ref, acc_ref):
    m_ref[...] = jnp.full(m_ref.shape, NEG_INF, F32)
    l_ref[...] = jnp.zeros(l_ref.shape, F32)
    acc_ref[...] = jnp.zeros(acc_ref.shape, F32)


def _attn_kernel(lam_ref, qt_ref, k_ref, vt_ref, band_ref, gd_ref, o_ref, m_ref, l_ref, acc_ref, *, tq):
    i = pl.program_id(2)
    qt = qt_ref[0, 0, 0]
    _softmax_init(m_ref, l_ref, acc_ref)

    def tile(j, bias):
        s = _dot(k_ref[0, j], qt)
        if bias is not None:
            s = s + jnp.concatenate([bias, bias], axis=1)
        m_old = m_ref[...]
        m_new = jnp.maximum(m_old, jnp.max(s, axis=0, keepdims=True))
        alpha = jnp.exp(m_old - m_new)
        p = jnp.exp(s - m_new)
        l_ref[...] = alpha * l_ref[...] + jnp.sum(p, axis=0, keepdims=True)
        acc_ref[...] = alpha * acc_ref[...] + _dot(vt_ref[0, 0, j], p.astype(BF16))
        m_ref[...] = m_new

    def far(j, carry):
        tile(j, None)
        return carry

    lax.fori_loop(0, jnp.maximum(i - 1, 0), far, 0)

    @pl.when(i >= 1)
    def _():
        tile(i - 1, band_ref[0, 1])

    tile(i, band_ref[0, 0])

    o = acc_ref[...] / l_ref[...]
    o = o[:, :tq] - lam_ref[0] * o[:, tq:]
    o = o * lax.rsqrt(jnp.mean(o * o, axis=0, keepdims=True) + RMS_EPS) * gd_ref[...] * (1.0 - LAM_INIT)
    o_ref[0] = o.T.astype(BF16)


def prompt_attention(lam, q2, kb, vb, band_t, gd, batch, seq, tq):
    nq = seq // tq
    head = 2 * DH
    qt = jnp.transpose(q2.reshape(2, batch, nq, tq, H_DIFF, head), (1, 4, 2, 5, 0, 3)).reshape(
        batch, H_DIFF, nq, head, 2 * tq)
    k4 = kb.reshape(batch, nq, tq, D_DIFF)
    vt = jnp.transpose(vb.reshape(batch, nq, tq, H_DIFF, head), (0, 3, 1, 4, 2))
    return pl.pallas_call(
        functools.partial(_attn_kernel, tq=tq),
        grid=(batch, H_DIFF, nq),
        in_specs=[pl.BlockSpec(memory_space=pltpu.SMEM),
                  pl.BlockSpec((1, 1, 1, head, 2 * tq), lambda b, h, i: (b, h, i, 0, 0)),
                  pl.BlockSpec((1, nq, tq, head), lambda b, h, i: (b, 0, 0, h)),
                  pl.BlockSpec((1, 1, nq, head, tq), lambda b, h, i: (b, h, 0, 0, 0)),
                  pl.BlockSpec((1, 2, tq, tq), lambda b, h, i: (h, 0, 0, 0)),
                  _const_spec((head, 1))],
        out_specs=pl.BlockSpec((1, tq, head), lambda b, h, i: (b, i, h)),
        out_shape=jax.ShapeDtypeStruct((batch, seq, D_DIFF), BF16),
        scratch_shapes=[pltpu.VMEM((1, 2 * tq), F32), pltpu.VMEM((1, 2 * tq), F32), pltpu.VMEM((head, 2 * tq), F32)],
        compiler_params=_params("parallel", "parallel", "arbitrary"),
        name="prompt_attention",
    )(lam, qt, k4, vt, band_t, gd.reshape(head, 1))


DEC_ROWS = 2 * H_DIFF


def _dec_attn_kernel(pt_ref, lam_ref, q_ref, *refs, n_groups, n_q, group):
    k_refs, v_refs = refs[:group], refs[group:2 * group]
    kn_ref, vn_ref, band_ref, expand_ref, gd_ref, o_ref, m_ref, l_ref, acc_ref = refs[2 * group:]
    j = pl.program_id(1)
    q = q_ref[0]
    page = band_ref.shape[2]

    @pl.when(j == 0)
    def _():
        _softmax_init(m_ref, l_ref, acc_ref)

    def update(s, weighted_values):
        m_old = m_ref[...]
        m_new = jnp.maximum(m_old, jnp.max(s, axis=-1, keepdims=True))
        alpha = jnp.exp(m_old - m_new)
        p = jnp.exp(s - m_new)
        l_ref[...] = alpha * l_ref[...] + jnp.sum(p, axis=-1, keepdims=True)
        acc_ref[...] = alpha * acc_ref[...] + weighted_values(p.astype(BF16))
        m_ref[...] = m_new

    scores = [_dot(q, k_ref[0].reshape(D_DIFF, page).astype(BF16)) for k_ref in k_refs]
    scores[-1] = scores[-1] + jnp.where(j == n_groups - 1, band_ref[0], 0.0)

    rows = q.shape[0]
    row_head = (lax.broadcasted_iota(I32, (rows, page * H_DIFF), 0) % (H_DIFF * n_q)) // n_q
    own_rows = (lax.broadcasted_iota(I32, (rows, page * H_DIFF), 1) % H_DIFF) == row_head

    def cached_values(p):
        o = None
        for u, v_ref in enumerate(v_refs):
            spread = _dot(p[:, u * page:(u + 1) * page], expand_ref[...])
            part = _dot(jnp.where(own_rows, spread, 0.0).astype(BF16), v_ref[0].astype(BF16))
            o = part if o is None else o + part
        return o

    update(jnp.concatenate(scores, axis=1), cached_values)

    @pl.when(j == n_groups - 1)
    def _():
        def new_values(p):
            full = _dot(p, vn_ref[0])
            head = (lax.broadcasted_iota(I32, (rows, 2 * DH), 0) % (H_DIFF * n_q)) // n_q
            return sum(jnp.where(head == h, full[:, h * 2 * DH:(h + 1) * 2 * DH], 0.0) for h in range(H_DIFF))

        update(_dot_nt(q, kn_ref[0]) + band_ref[1], new_values)
        o = acc_ref[...] / l_ref[...]
        half = rows // 2
        o = o[:half] - lam_ref[0] * o[half:]
        o = _rms(o, gd_ref[...]) * (1.0 - LAM_INIT)
        o_ref[0] = jnp.concatenate([o[h * n_q:(h + 1) * n_q] for h in range(H_DIFF)], axis=1).astype(BF16)


PAGE_GROUP = 8


def sample_attention(page_table, lam, qst, cache_kt, cache_v, kn, vn, band, gd, n_q):
    b, n_pages = page_table.shape
    page = cache_kt.shape[-1]
    rows = n_q * DEC_ROWS
    expand = jnp.repeat(jnp.eye(page, dtype=BF16), H_DIFF, axis=1)
    group = math.gcd(PAGE_GROUP, n_pages)
    n_groups = n_pages // group

    def page_of(u):
        return lambda i, j, pt: (pt[i, j * group + u],)

    k_specs = [pl.BlockSpec((1,) + cache_kt.shape[1:], lambda i, j, pt, f=page_of(u): f(i, j, pt) + (0, 0, 0, 0))
               for u in range(group)]
    v_specs = [pl.BlockSpec((1,) + cache_v.shape[1:], lambda i, j, pt, f=page_of(u): f(i, j, pt) + (0, 0))
               for u in range(group)]
    new_spec = pl.BlockSpec((1, page, D_DIFF), lambda i, j, pt: (i, 0, 0))
    grid_spec = pltpu.PrefetchScalarGridSpec(
        num_scalar_prefetch=1,
        grid=(b, n_groups),
        in_specs=[pl.BlockSpec(memory_space=pltpu.SMEM),
                  pl.BlockSpec((1, rows, D_DIFF), lambda i, j, pt: (i, 0, 0))]
                 + k_specs + v_specs
                 + [new_spec, new_spec,
                    pl.BlockSpec((2, rows, page), lambda i, j, pt: (0, 0, 0)),
                    pl.BlockSpec(expand.shape, lambda i, j, pt: (0, 0)),
                    pl.BlockSpec((1, 2 * DH), lambda i, j, pt: (0, 0))],
        out_specs=pl.BlockSpec((1, n_q, D_DIFF), lambda i, j, pt: (i, 0, 0)),
        scratch_shapes=[pltpu.VMEM((rows, 1), F32), pltpu.VMEM((rows, 1), F32), pltpu.VMEM((rows, 2 * DH), F32)],
    )
    return pl.pallas_call(
        functools.partial(_dec_attn_kernel, n_groups=n_groups, n_q=n_q, group=group),
        grid_spec=grid_spec,
        out_shape=jax.ShapeDtypeStruct((b, n_q, D_DIFF), BF16),
        compiler_params=_params("parallel", "arbitrary"),
        name="sample_attention",
    )(page_table, lam, qst, *([cache_kt] * group), *([cache_v] * group), kn, vn, band, expand, gd)


def _mid_kernel(x_ref, ys_ref, yd_ref, wa_ref, wb_ref, g_ref, wq_ref, h_ref, m_ref, qp_ref):
    h = x_ref[...] + _dot(ys_ref[...], wa_ref[...]) + _dot(yd_ref[...], wb_ref[...])
    h_ref[...] = h
    m = _rms(h, g_ref[...]).astype(BF16)
    mf = m.astype(F32)
    for c in range(D_MODEL // LANES):
        m_ref[:, c, :] = mf[:, c * LANES:(c + 1) * LANES]
    qp_ref[...] = _dot(m, wq_ref[...]).astype(BF16)


def mid(x, ys, yd, wa, wb, g, wq):
    t = x.shape[0]
    tm = min(ROW_TILE, t)
    row = lambda n: pl.BlockSpec((tm, n), lambda i: (i, 0))
    nq = wq.shape[1]
    return pl.pallas_call(
        _mid_kernel,
        grid=(t // tm,),
        in_specs=[row(D_MODEL), row(D_SSM), row(D_DIFF), _const_spec(wa.shape), _const_spec(wb.shape),
                  _const_spec((1, D_MODEL)), _const_spec(wq.shape)],
        out_specs=(row(D_MODEL), pl.BlockSpec((tm, D_MODEL // LANES, LANES), lambda i: (i, 0, 0)), row(nq)),
        out_shape=(jax.ShapeDtypeStruct((t, D_MODEL), F32), jax.ShapeDtypeStruct((t, D_MODEL // LANES, LANES), F32),
                   jax.ShapeDtypeStruct((t, nq), BF16)),
        compiler_params=_params("parallel"),
        name="mid",
    )(x, ys, yd, wa, wb, g, wq)


def _top16(vals):
    n = vals.shape[0]
    half = n // 2
    from_end = (n - 1 - lax.broadcasted_iota(I32, vals.shape, 0)).astype(F32)
    a, b = vals[:half], vals[half:]
    ia, ib = from_end[:half], from_end[half:]
    a_wins = a >= b
    win, win_i = jnp.where(a_wins, a, b), jnp.where(a_wins, ia, ib)
    lose, lose_i = jnp.where(a_wins, b, a), jnp.where(a_wins, ib, ia)
    out_v, out_i = [], []
    for _ in range(PEER_TOPK):
        mx = jnp.max(win, axis=0, keepdims=True)
        at = jnp.max(jnp.where(win == mx, win_i, -1.0), axis=0, keepdims=True)
        out_v.append(mx)
        out_i.append(at)
        taken = win_i == at
        win, win_i, lose = jnp.where(taken, lose, win), jnp.where(taken, lose_i, win_i), jnp.where(taken, -jnp.inf, lose)
    return jnp.concatenate(out_v, axis=0), n - 1 - jnp.concatenate(out_i, axis=0).astype(I32)


HI_MASK = -65536
LOW_HALF_SHIFT = 16


def _topk_kernel(qp_ref, sk_ref, row_ref, shift_ref, shift_t_ref, gate_ref):
    tb = qp_ref.shape[0]
    half = D_KEY // 2
    idx_rows, gate_rows = [], []
    for h in range(PEER_HEADS):
        s, ix = [], []
        for c in range(2):
            qh = qp_ref[:, (2 * h + c) * half:(2 * h + c + 1) * half]
            sc = _dot_nt(sk_ref[c], qh)
            v, i = _top16(sc)
            s.append(v)
            ix.append(i)
        pieces = [(a, PEER_TOPK if a == 0 else SUBLANES) for a in range(SUBLANES)]
        cand = jnp.concatenate([s[0][a:a + 1, :] + s[1][:nb] for a, nb in pieces] + [s[0][SUBLANES:] + s[1][0:1]],
                               axis=0)
        cidx = jnp.concatenate([ix[0][a:a + 1, :] * N_KEYS + ix[1][:nb] for a, nb in pieces]
                               + [ix[0][SUBLANES:] * N_KEYS + ix[1][0:1]], axis=0)
        sc, pos = _top16(cand)
        prow = lax.broadcasted_iota(I32, cand.shape, 0)
        cidx_f = cidx.astype(F32)
        idx = jnp.concatenate(
            [jnp.max(jnp.where(prow == pos[k:k + 1, :], cidx_f, -1.0), axis=0, keepdims=True)
             for k in range(PEER_TOPK)], axis=0).astype(I32)
        e = jnp.exp(sc - sc[0:1, :])
        gate_rows.append(e / jnp.sum(e, axis=0, keepdims=True))
        idx_rows.append(idx)
    idx_t = jnp.concatenate(idx_rows, axis=0)
    shift_t = ((idx_t & 1) ^ 1) * LOW_HALF_SHIFT
    shift_t_ref[0] = shift_t.astype(F32)
    idx = idx_t.T
    gate_ref[...] = jnp.concatenate(gate_rows, axis=0).T
    row_ref[...] = (idx >> 1) * SUBLANES
    shift_ref[...] = shift_t.T


def peer_topk(qp, sk):
    t = qp.shape[0]
    tb = min(PEER_TILE, t)
    out = pl.BlockSpec((tb, SLOTS), lambda i: (i, 0))
    return pl.pallas_call(
        _topk_kernel,
        grid=(t // tb,),
        in_specs=[pl.BlockSpec((tb, qp.shape[1]), lambda i: (i, 0)), _const_spec(sk.shape)],
        out_specs=(out, out, pl.BlockSpec((1, SLOTS, tb), lambda i: (i, 0, 0)), out),
        out_shape=(jax.ShapeDtypeStruct((t, SLOTS), I32), jax.ShapeDtypeStruct((t, SLOTS), I32),
                   jax.ShapeDtypeStruct((t // tb, SLOTS, tb), F32), jax.ShapeDtypeStruct((t, SLOTS), F32)),
        compiler_params=_params("parallel"),
        name="peer_topk",
    )(qp, sk)


def _pack_kernel(*refs):
    w_refs, o_ref = refs[:-1], refs[-1]
    chunks = len(w_refs)
    n = o_ref.shape[0] // chunks
    for c, w_ref in enumerate(w_refs):
        even = w_ref[pl.ds(0, n, stride=2), :].astype(BF16).astype(F32)
        odd = w_ref[pl.ds(1, n, stride=2), :].astype(BF16).astype(F32)
        lo = (lax.bitcast_convert_type(even, I32) >> 16) & 0xFFFF
        o_ref[pl.ds(c, n, stride=chunks), :] = lo | (lax.bitcast_convert_type(odd, I32) & HI_MASK)


def pack_table(w):
    n, d = w.shape
    rows = 512
    chunks = d // LANES
    return pl.pallas_call(
        _pack_kernel,
        grid=(n // (2 * rows),),
        in_specs=[pl.BlockSpec((2 * rows, LANES), lambda i, c=c: (i, c)) for c in range(chunks)],
        out_specs=pl.BlockSpec((rows * chunks, LANES), lambda i: (i, 0)),
        out_shape=jax.ShapeDtypeStruct((n // 2 * chunks, LANES), I32),
        compiler_params=_params("parallel"),
        name="pack_table",
    )(*([w] * chunks))


def _expert_row(tab_ref, row, shift):
    word = tab_ref[pl.ds(pl.multiple_of(row, SUBLANES), SUBLANES), :]
    return lax.bitcast_convert_type((word << shift) & HI_MASK, F32)


UNIT = SUBLANES


def _butterfly(parts, add, roll, select):
    def node(first, count, offset):
        if count == 1:
            return parts[first]
        run = SUBLANES // count
        half = count // 2
        o1 = offset % (2 * run)
        a = node(first, half, o1)
        b = node(first + half, half, (o1 + run) % (2 * run))
        takes_a = lambda p: ((p - o1) % (2 * run)) >= run
        return add(select(takes_a, a, b), roll(select(takes_a, b, a), run))

    return node(0, SUBLANES, 0)


def _butterfly_order():
    tiles = [[{p}] * SUBLANES for p in range(SUBLANES)]
    out = _butterfly(
        tiles,
        add=lambda a, b: [x | y for x, y in zip(a, b)],
        roll=lambda a, k: [a[(i - k) % SUBLANES] for i in range(SUBLANES)],
        select=lambda takes_a, a, b: [a[i] if takes_a(i) else b[i] for i in range(SUBLANES)])
    order = [0] * SUBLANES
    for sublane, owners in enumerate(out):
        (p,) = owners
        order[p] = sublane
    return order


BUTTERFLY_ORDER = _butterfly_order()


def _rows_to_sublanes(parts):
    sub = lax.broadcasted_iota(I32, (SUBLANES, LANES), 0)
    placed = [None] * SUBLANES
    for i, part in enumerate(parts):
        placed[BUTTERFLY_ORDER.index(i)] = part
    return _butterfly(placed, add=lambda a, b: a + b, roll=lambda a, k: pltpu.roll(a, k, 0),
                      select=lambda takes_a, a, b: jnp.where(takes_a(sub), a, b))


def _splat(ref, j, t):
    return jnp.broadcast_to(ref[j:j + 1, t * LANES:(t + 1) * LANES], (SUBLANES, LANES))


def _peer_u_kernel(row_ref, m_ref, gate_ref, shiftv_ref, shift_t_ref, spread_ref, tab_ref, wt_ref,
                   part_ref, h_ref, pad_ref, shift_splat_ref):
    tb = m_ref.shape[0]
    step = pl.program_id(0)

    @pl.when(step == 0)
    def _():
        pad_ref[...] = jnp.zeros(pad_ref.shape, F32)

    picks = spread_ref[step % spread_ref.shape[0]]
    shift_splat_ref[...] = _dot(shift_t_ref[0].astype(BF16), picks).astype(I32)

    def token(t):
        x = m_ref[t]
        for u in range(SLOTS // UNIT):
            parts = [_expert_row(tab_ref, row_ref[t, j], _splat(shift_splat_ref, j, t)) * x
                     for j in range(u * UNIT, (u + 1) * UNIT)]
            part_ref[t, u * UNIT:(u + 1) * UNIT, :] = _rows_to_sublanes(parts)

    for t in range(tb):
        token(t)
    for t in range(tb):
        h_ref[t:t + 1, :] = jnp.sum(part_ref[t].T, axis=0, keepdims=True)
    h = h_ref[...]
    act = 0.5 * h * (1.0 + lax.erf(h * (2.0 ** -0.5)))
    pad_ref[0:tb, :] = (gate_ref[...] * act).astype(BF16).astype(F32)
    pad_ref[tb:2 * tb, :] = shiftv_ref[...].astype(F32)
    wt_ref[0] = pad_ref[...].T


def _token_picks(n_tokens, tb):
    eye = jnp.eye(n_tokens, dtype=BF16).reshape(n_tokens, n_tokens // tb, tb)
    return jnp.repeat(jnp.moveaxis(eye, 1, 0), LANES, axis=2)


def peer_u(row, shift, shift_t, m3, gate, tab):
    t = m3.shape[0]
    tb = min(EXPERT_TILE, t)
    top_tb = shift_t.shape[2]
    per_top = top_tb // tb
    picks = _token_picks(top_tb, tb)
    smem = pl.BlockSpec((tb, SLOTS), lambda i: (i, 0), memory_space=pltpu.SMEM)
    vmem = pl.BlockSpec((tb, SLOTS), lambda i: (i, 0))
    return pl.pallas_call(
        _peer_u_kernel,
        grid=(t // tb,),
        in_specs=[smem, pl.BlockSpec((tb, SUBLANES, LANES), lambda i: (i, 0, 0)), vmem, vmem,
                  pl.BlockSpec((1, SLOTS, top_tb), lambda i: (i // per_top, 0, 0)), _const_spec(picks.shape),
                  pl.BlockSpec(tab.shape, lambda i: (0, 0), pipeline_mode=pl.Buffered(1))],
        out_specs=pl.BlockSpec((1, SLOTS, LANES), lambda i: (i, 0, 0)),
        out_shape=jax.ShapeDtypeStruct((t // tb, SLOTS, LANES), F32),
        scratch_shapes=[pltpu.VMEM((tb, SLOTS, LANES), F32), pltpu.VMEM((tb, SLOTS), F32),
                        pltpu.VMEM((LANES, SLOTS), F32), pltpu.VMEM((SLOTS, tb * LANES), I32)],
        compiler_params=_params("arbitrary"),
        name="peer_u",
    )(row, m3, gate, shift, shift_t, picks, tab)


N_ACC = 8


REGION_TOKENS = 4


def _token_regions(n, body):
    always = pl.program_id(0) >= 0

    def region(first):
        for t in range(first, min(first + REGION_TOKENS, n)):
            body(t)

    for first in range(0, n, REGION_TOKENS):
        pl.when(always)(functools.partial(region, first))


def _peer_v_kernel(row_ref, wt_ref, spread_ref, tab_ref, o_ref, code_ref):
    tb = o_ref.shape[0]
    spread = _dot(wt_ref[0][:, :2 * tb].astype(BF16), spread_ref[...])
    code_ref[...] = (lax.bitcast_convert_type(spread[:, :tb * LANES], I32)
                     | spread[:, tb * LANES:].astype(I32))

    def token(t):
        acc = [None] * N_ACC
        for j in range(SLOTS):
            code = _splat(code_ref, j, t)
            weight = lax.bitcast_convert_type(code & HI_MASK, F32)
            term = weight * _expert_row(tab_ref, row_ref[t, j], code & LOW_HALF_SHIFT)
            acc[j % N_ACC] = term if acc[j % N_ACC] is None else acc[j % N_ACC] + term
        while len(acc) > 1:
            acc = [a + b for a, b in zip(acc[0::2], acc[1::2])]
        o_ref[t] = acc[0]

    _token_regions(tb, token)


def peer_v(row, wt, tab):
    t = row.shape[0]
    tb = min(EXPERT_TILE, t)
    smem = pl.BlockSpec((tb, SLOTS), lambda i: (i, 0), memory_space=pltpu.SMEM)
    spread = jnp.repeat(jnp.eye(2 * tb, dtype=BF16), LANES, axis=1)
    return pl.pallas_call(
        _peer_v_kernel,
        grid=(t // tb,),
        in_specs=[smem, pl.BlockSpec((1, SLOTS, LANES), lambda i: (i, 0, 0)), _const_spec(spread.shape),
                  pl.BlockSpec(tab.shape, lambda i: (0, 0), pipeline_mode=pl.Buffered(1))],
        out_specs=pl.BlockSpec((tb, SUBLANES, LANES), lambda i: (i, 0, 0)),
        out_shape=jax.ShapeDtypeStruct((t, SUBLANES, LANES), F32),
        scratch_shapes=[pltpu.VMEM((SLOTS, tb * LANES), I32)],
        compiler_params=_params("arbitrary"),
        name="peer_v",
    )(row, wt, spread, tab)


def _ple_kernel(h_ref, peer_ref, p_ref, wp_ref, wg_ref, gp_ref, gf_ref, y_ref):
    peer = jnp.concatenate([peer_ref[:, c, :] for c in range(D_MODEL // LANES)], axis=1)
    h = h_ref[...] + peer
    gate = jax.nn.sigmoid(_dot(_rms(h, gp_ref[...]).astype(BF16), wg_ref[...]))
    h = h + _dot(p_ref[...].astype(BF16), wp_ref[...]) * gate
    y_ref[...] = _rms(h, gf_ref[...])


def ple(h, peer_out, p, wp, wg, gp, gf):
    t = h.shape[0]
    tm = min(ROW_TILE, t)
    row = lambda n: pl.BlockSpec((tm, n), lambda i: (i, 0))
    return pl.pallas_call(
        _ple_kernel,
        grid=(t // tm,),
        in_specs=[row(D_MODEL), pl.BlockSpec((tm, D_MODEL // LANES, LANES), lambda i: (i, 0, 0)), row(PLE_DIM),
                  _const_spec(wp.shape), _const_spec(wg.shape),
                  _const_spec((1, D_MODEL)), _const_spec((1, D_MODEL))],
        out_specs=row(D_MODEL),
        out_shape=jax.ShapeDtypeStruct((t, D_MODEL), F32),
        compiler_params=_params("parallel"),
        name="ple",
    )(h, peer_out, p, wp, wg, gp, gf)


def _t5_bucket(rel):
    n = jnp.maximum(rel, 0)
    max_exact = N_BUCKETS // 2
    nf = jnp.maximum(n, 1).astype(F32)
    large = max_exact + (jnp.log(nf / max_exact) / math.log(REL_MAX_DIST / max_exact)
                         * (N_BUCKETS - max_exact)).astype(I32)
    large = jnp.minimum(large, N_BUCKETS - 1)
    return jnp.where(n < max_exact, n, large)


def _band_bias(rel_bias, rel):
    far = rel_bias[N_BUCKETS - 1]
    bucket = _t5_bucket(rel)[..., None]
    b = jnp.zeros(rel.shape + (H_DIFF,), F32)
    for n in range(N_BUCKETS - 1):
        b = jnp.where(bucket == n, rel_bias[n] - far, b)
    b = jnp.where((rel >= 0)[..., None], b, NEG_INF)
    return jnp.moveaxis(b, -1, 0).astype(F32)


def _pad_rows(a, rows):
    return jnp.pad(a, ((0, 0), (0, rows - a.shape[1]), (0, 0)))


def _layer(x, p_l, conv_buf, ssm0, lam, w, attention):
    b, l, _ = x.shape
    t = b * l
    x2 = x.reshape(t, D_MODEL)
    z, xbc, q2, k, v, kb, vb, dt = in_proj(x2, w["g_mix"], w["wz"], w["wx"], w["wq"], w["wk"], w["wv"], w["wdt"], l)
    if k.ndim == 3:
        k = jnp.transpose(k.reshape(b, H_DIFF, 2, DH, l), (0, 4, 1, 2, 3))
    else:
        k = k.reshape(b, l, H_DIFF, 2, DH)

    lp = -(-l // SSD_CHUNK) * SSD_CHUNK
    xbc3 = xbc.reshape(b, l, CONV_DIM)
    cbuf = jnp.pad(conv_buf, ((0, 0), (SUBLANES - (CONV_W - 1), 0), (0, 0)))
    y_ssm, ssm_final = ssd(_pad_rows(xbc3, lp), _pad_rows(z.reshape(b, l, D_SSM), lp),
                           _pad_rows(dt.reshape(b, l, LANES), lp), cbuf, ssm0,
                           w["conv_w"], w["conv_b"], w["dt_bias"], w["a_log"], w["d_skip"], w["g_ssm_norm"], l)
    y_ssm = y_ssm[:, :l].reshape(t, D_SSM)
    tail = CONV_W - 1
    conv_new = xbc3[:, l - tail:] if l >= tail else jnp.concatenate([conv_buf, xbc3], axis=1)[:, -tail:]

    y_diff = attention(q2, kb, vb)

    h1, m, qp = mid(x2, y_ssm, y_diff, w["wo_a"], w["wo_b"], w["g_ffn"], w["w_query"])
    row, shift, shift_t, gate = peer_topk(qp, w["sub_keys"])
    wt = peer_u(row, shift, shift_t, m, gate, w["tab_u"])
    peer_out = peer_v(row, wt, w["tab_v"])
    y = ple(h1, peer_out, p_l.reshape(t, PLE_DIM), w["w_ple"], w["w_ple_gate"], w["g_ple"], w["g_final"])
    return (y.reshape(b, l, D_MODEL), k, v.reshape(b, l, H_DIFF, 2 * DH),
            ssm_final, conv_new)


def kernel(x_prompt, x_sample, cache_k, cache_v, state_ssm, state_conv, page_table, p_prompt, p_sample, g_mix, w_in, conv_w, conv_b, dt_bias, a_log, d_skip, g_ssm_norm, lam_q1, lam_k1, lam_q2, lam_k2, g_diff_norm, rel_bias, w_out, g_ffn, w_query, sub_keys, expert_u, expert_v, w_ple, w_ple_gate, g_ple, g_final):
    assert w_in.shape[0] == 1, "single-layer problem"
    bsz, seq, _ = x_prompt.shape
    dec_b, dec_l, _ = x_sample.shape
    n_pages = page_table.shape[1]
    page = cache_k.shape[2]
    past = n_pages * page

    wi = w_in[0].astype(BF16)
    o_x = D_SSM
    o_dt = o_x + CONV_DIM
    o_q = o_dt + H_SSM
    o_k = o_q + D_DIFF
    o_v = o_k + D_DIFF
    lanes_of = lambda a: jnp.pad(a.astype(F32)[None, :], ((0, 0), (0, LANES - a.shape[0])))
    w = dict(
        g_mix=g_mix[0][None], wz=wi[:, :o_x], wx=wi[:, o_x:o_dt],
        wdt=jnp.pad(wi[:, o_dt:o_q], ((0, 0), (0, LANES - H_SSM))),
        wq=wi[:, o_q:o_k], wk=wi[:, o_k:o_v], wv=wi[:, o_v:],
        conv_w=conv_w[0], conv_b=conv_b[0][None], dt_bias=lanes_of(dt_bias[0]), a_log=lanes_of(a_log[0]),
        d_skip=lanes_of(d_skip[0]), g_ssm_norm=g_ssm_norm[0][None],
        wo_a=w_out[0][:D_SSM].astype(BF16), wo_b=w_out[0][D_SSM:].astype(BF16), g_ffn=g_ffn[0][None],
        w_query=w_query[0].astype(BF16), sub_keys=sub_keys[0].astype(BF16),
        tab_u=pack_table(expert_u[0]), tab_v=pack_table(expert_v[0]),
        w_ple=w_ple[0].astype(BF16), w_ple_gate=w_ple_gate[0].astype(BF16), g_ple=g_ple[0][None],
        g_final=g_final[None],
    )
    gd = g_diff_norm[0][None]
    lam = (jnp.exp(jnp.sum(lam_q1[0] * lam_k1[0])) - jnp.exp(jnp.sum(lam_q2[0] * lam_k2[0])) + LAM_INIT).reshape(1)

    tq = min(ATTN_TILE, seq)
    ii = jnp.arange(tq, dtype=I32)
    assert tq >= REL_MAX_DIST and page >= REL_MAX_DIST, "bias is only applied next to the diagonal"
    rel = jnp.stack([ii[None, :] - ii[:, None], tq + ii[None, :] - ii[:, None]])
    band_p = _band_bias(rel_bias, rel)

    def attn_prompt(q2, kb, vb):
        return prompt_attention(lam, q2, kb, vb, band_p, gd, bsz, seq, tq).reshape(bsz * seq, D_DIFF)

    yp, kp, vp, sp, cp = _layer(x_prompt, p_prompt[0], jnp.zeros((bsz, CONV_W - 1, CONV_DIM), F32),
                                jnp.zeros((bsz, H_SSM, P_SSM, N_STATE), F32), lam, w, attn_prompt)

    rows = dec_l * DEC_ROWS
    r = jnp.arange(rows, dtype=I32)
    qpos = past + r % dec_l
    head_of = (r % (H_DIFF * dec_l)) // dec_l
    map_of = r // (H_DIFF * dec_l)
    kk = jnp.arange(page, dtype=I32)
    rel_last = qpos[:, None] - (past - page + kk)[None, :]
    rel_new = jnp.where(kk[None, :] < dec_l, qpos[:, None] - (past + kk)[None, :], -1)
    band_all = _band_bias(rel_bias, jnp.stack([rel_last, rel_new]))
    band_s = jnp.take_along_axis(band_all, head_of[None, None, :, None], axis=0)[0]
    lane_slot = jnp.arange(D_DIFF, dtype=I32) // DH
    q_mask = (lane_slot[None, :] == (head_of * 2 + map_of)[:, None])
    cache_v2 = cache_v[0].reshape(-1, page * H_DIFF, 2 * DH)

    def attn_sample(q2, kb, vb):
        q = (q2[0] + q2[1]).reshape(dec_b, dec_l, D_DIFF)
        qst = jnp.where(q_mask[None], jnp.tile(q, (1, DEC_ROWS, 1)), jnp.zeros((), BF16))
        kn = _pad_rows(kb.reshape(dec_b, dec_l, D_DIFF), page)
        vn = _pad_rows(vb.reshape(dec_b, dec_l, D_DIFF), page)
        o = sample_attention(page_table, lam, qst, jnp.transpose(cache_k[0], (0, 2, 3, 4, 1)), cache_v2,
                             kn, vn, band_s, gd, dec_l)
        return o.reshape(dec_b * dec_l, D_DIFF)

    ys, ks, vs, ss, cs = _layer(x_sample, p_sample[0], state_conv[0], state_ssm[0], lam, w, attn_sample)

    return (yp, ys, kp[None], vp[None], sp[None], cp[None], ks[None], vs[None], ss[None], cs[None])
```

```python
import functools
import math

import jax
import jax.numpy as jnp
from jax import lax
from jax.experimental import pallas as pl
from jax.experimental.pallas import tpu as pltpu

F32 = jnp.float32
BF16 = jnp.bfloat16
I32 = jnp.int32

D_MODEL = 1024
D_SSM = 512
D_DIFF = 512
P_SSM = 64
H_SSM = 8
N_STATE = 128
SSM_GROUPS = 2
HEADS_PER_GROUP = H_SSM // SSM_GROUPS
CONV_W = 4
CONV_DIM = D_SSM + 2 * SSM_GROUPS * N_STATE
DH = 64
H_DIFF = 4
N_BUCKETS = 32
REL_MAX_DIST = 128
N_KEYS = 128
PEER_HEADS = 8
PEER_TOPK = 16
D_KEY = 256
PLE_DIM = 256
NEG_INF = -1e30
RMS_EPS = 1e-6
LAM_INIT = 0.8 - 0.6 * math.exp(-0.3 * 0)

LANES = 128
SUBLANES = 8
VMEM_LIMIT = 56 * 1024 * 1024

SSD_CHUNK = 128
ROW_TILE = 256
ATTN_TILE = 1024
PEER_TILE = 128
EXPERT_TILE = 32
SLOTS = PEER_HEADS * PEER_TOPK


def _params(*sem):
    return pltpu.CompilerParams(dimension_semantics=sem, vmem_limit_bytes=VMEM_LIMIT)


def _const_spec(shape):
    nd = len(shape)
    return pl.BlockSpec(shape, lambda *_: (0,) * nd)


def _rms(x, g):
    return x * lax.rsqrt(jnp.mean(x * x, axis=-1, keepdims=True) + RMS_EPS) * g


def _dot(a, b):
    return jnp.dot(a, b, preferred_element_type=F32)


def _dot_nt(a, b):
    return lax.dot_general(a, b, (((1,), (1,)), ((), ())), preferred_element_type=F32)


def _dot_tn(a, b):
    return lax.dot_general(a, b, (((0,), (0,)), ((), ())), preferred_element_type=F32)


def _in_proj_kernel(x_ref, g_ref, wz_ref, wx_ref, wq_ref, wk_ref, wv_ref, wdt_ref,
                    z_ref, xbc_ref, q2_ref, k_ref, v_ref, kb_ref, vb_ref, dt_ref, *t_refs, transposed_k):
    a = _rms(x_ref[...], g_ref[...]).astype(BF16)
    z_ref[...] = _dot(a, wz_ref[...])
    xbc_ref[...] = _dot(a, wx_ref[...])
    dt_ref[...] = _dot(a, wdt_ref[...])
    k = _dot(a, wk_ref[...])
    v = _dot(a, wv_ref[...])
    if transposed_k:
        k_ref[0] = k.T
    else:
        k_ref[...] = k
    for h in range(H_DIFF):
        v_ref[:, h, :] = v[:, h * 2 * DH:(h + 1) * 2 * DH]
    kb_ref[...] = k.astype(BF16)
    vb_ref[...] = v.astype(BF16)
    q = _dot(a, wq_ref[...]).astype(BF16) * (DH ** -0.5)
    lane = lax.broadcasted_iota(I32, q.shape, 1)
    first_map = (lane % (2 * DH)) < DH
    zero = jnp.zeros_like(q)
    q2_ref[0] = jnp.where(first_map, q, zero)
    q2_ref[1] = jnp.where(first_map, zero, q)
    if transposed_k:
        qa_ref, qb_ref, vt_ref = t_refs
        q_t = q.astype(F32).T
        row = lax.broadcasted_iota(I32, q_t.shape, 0)
        first_rows = (row % (2 * DH)) < DH
        qa_ref[0, 0] = jnp.where(first_rows, q_t, 0.0).astype(BF16)
        qb_ref[0, 0] = jnp.where(first_rows, 0.0, q_t).astype(BF16)
        vt_ref[0, 0] = v.T.astype(BF16)


def in_proj(x, g, wz, wx, wq, wk, wv, wdt, seq, tq):
    t = x.shape[0]
    tm = min(ROW_TILE, t)
    row = lambda n: pl.BlockSpec((tm, n), lambda i: (i, 0))
    transposed_k = seq % tm == 0 and tq % tm == 0
    per_seq = seq // tm if transposed_k else 1
    extra_shapes, extra_specs = (), ()
    if transposed_k:
        k_shape = jax.ShapeDtypeStruct((t // seq, D_DIFF, seq), F32)
        k_spec = pl.BlockSpec((1, D_DIFF, tm), lambda i: (i // per_seq, 0, i % per_seq))
        per_tile = tq // tm
        t_shape = jax.ShapeDtypeStruct((t // seq, seq // tq, D_DIFF, tq), BF16)
        t_spec = pl.BlockSpec((1, 1, D_DIFF, tm),
                              lambda i: (i // per_seq, (i % per_seq) // per_tile, 0, (i % per_seq) % per_tile))
        extra_shapes, extra_specs = (t_shape,) * 3, (t_spec,) * 3
    else:
        k_shape, k_spec = jax.ShapeDtypeStruct((t, D_DIFF), F32), row(D_DIFF)
    out_shape = (
        jax.ShapeDtypeStruct((t, D_SSM), F32),
        jax.ShapeDtypeStruct((t, CONV_DIM), F32),
        jax.ShapeDtypeStruct((2, t, D_DIFF), BF16),
        k_shape,
        jax.ShapeDtypeStruct((t, H_DIFF, 2 * DH), F32),
        jax.ShapeDtypeStruct((t, D_DIFF), BF16),
        jax.ShapeDtypeStruct((t, D_DIFF), BF16),
        jax.ShapeDtypeStruct((t, LANES), F32),
    ) + extra_shapes
    return pl.pallas_call(
        functools.partial(_in_proj_kernel, transposed_k=transposed_k),
        grid=(t // tm,),
        in_specs=[row(D_MODEL), _const_spec((1, D_MODEL)), _const_spec(wz.shape), _const_spec(wx.shape),
                  _const_spec(wq.shape), _const_spec(wk.shape), _const_spec(wv.shape), _const_spec(wdt.shape)],
        out_specs=(row(D_SSM), row(CONV_DIM), pl.BlockSpec((2, tm, D_DIFF), lambda i: (0, i, 0)),
                   k_spec, pl.BlockSpec((tm, H_DIFF, 2 * DH), lambda i: (i, 0, 0)), row(D_DIFF), row(D_DIFF),
                   row(LANES)) + extra_specs,
        out_shape=out_shape,
        compiler_params=_params("parallel"),
        name="in_proj",
    )(x, g, wz, wx, wq, wk, wv, wdt)


def _ssd_kernel(xbc_ref, z_ref, dt_ref, cbuf_ref, s0_ref, cw_ref, cb_ref, dtb_ref, alog_ref, dskip_ref, gn_ref,
                y_ref, sfin_ref, win_ref, state_ref, *, seq_len):
    c = pl.program_id(1)
    lc = SSD_CHUNK

    @pl.when(c == 0)
    def _():
        win_ref[0:SUBLANES, :] = cbuf_ref[0]
        state_ref[...] = s0_ref[0]

    win_ref[SUBLANES:SUBLANES + lc, :] = xbc_ref[0]
    base = SUBLANES - (CONV_W - 1)
    conv = cb_ref[...]
    for j in range(CONV_W):
        conv = conv + cw_ref[j:j + 1, :] * win_ref[base + j:base + j + lc, :]
    win_ref[0:SUBLANES, :] = win_ref[lc:lc + SUBLANES, :]
    xc = conv * jax.nn.sigmoid(conv)

    pos = c * lc + lax.broadcasted_iota(I32, (lc, LANES), 0)
    dt = jax.nn.softplus(dt_ref[0] + dtb_ref[...])
    dt = jnp.where(pos < seq_len, dt, 0.0)
    adt = -jnp.exp(alog_ref[...]) * dt
    row = lax.broadcasted_iota(I32, (lc, lc), 0)
    col = lax.broadcasted_iota(I32, (lc, lc), 1)
    causal = row >= col
    tri = causal.astype(F32)
    cs = jnp.dot(tri, adt, preferred_element_type=F32, precision=lax.Precision.HIGHEST)
    cs_t = cs.T
    cs_last = cs[lc - 1:lc, :]

    ys = []
    for g in range(SSM_GROUPS):
        b0 = D_SSM + g * N_STATE
        c0 = D_SSM + SSM_GROUPS * N_STATE + g * N_STATE
        bg = xc[:, b0:b0 + N_STATE]
        cg = xc[:, c0:c0 + N_STATE].astype(BF16)
        cb = _dot_nt(cg, bg.astype(BF16))
        for hh in range(HEADS_PER_GROUP):
            h = g * HEADS_PER_GROUP + hh
            xs = xc[:, h * P_SSM:(h + 1) * P_SSM]
            cs_h = cs[:, h:h + 1]
            decay = jnp.where(causal, jnp.exp(jnp.where(causal, cs_h - cs_t[h:h + 1, :], 0.0)), 0.0)
            xdt = xs * dt[:, h:h + 1]
            y = _dot((cb * decay).astype(BF16), xdt.astype(BF16))
            st = state_ref[h]
            y = y + jnp.exp(cs_h) * _dot_nt(cg, st.astype(BF16))
            y = y + dskip_ref[:, h:h + 1] * xs
            ys.append(y)
            to_end = jnp.exp(cs_last[:, h:h + 1] - cs_h)
            new = _dot_tn((xdt * to_end).astype(BF16), bg.astype(BF16))
            state_ref[h] = jnp.exp(cs_last[:, h:h + 1]) * st + new
    y = jnp.concatenate(ys, axis=1)
    zz = z_ref[0]
    y = y * (zz * jax.nn.sigmoid(zz))
    gw = D_SSM // SSM_GROUPS
    outs = []
    for g in range(SSM_GROUPS):
        yg = y[:, g * gw:(g + 1) * gw]
        outs.append(yg * lax.rsqrt(jnp.mean(yg * yg, axis=-1, keepdims=True) + RMS_EPS))
    y_ref[0] = (jnp.concatenate(outs, axis=1) * gn_ref[...]).astype(BF16)

    @pl.when(c == pl.num_programs(1) - 1)
    def _():
        sfin_ref[0] = state_ref[...]


def ssd(xbc, z, dt, cbuf, s0, cw, cb, dtb, alog, dskip, gn, seq_len):
    b, lp, _ = xbc.shape
    lc = SSD_CHUNK
    seq = lambda n: pl.BlockSpec((1, lc, n), lambda i, c: (i, c, 0))
    return pl.pallas_call(
        functools.partial(_ssd_kernel, seq_len=seq_len),
        grid=(b, lp // lc),
        in_specs=[seq(CONV_DIM), seq(D_SSM), seq(LANES),
                  pl.BlockSpec((1, SUBLANES, CONV_DIM), lambda i, c: (i, 0, 0)),
                  pl.BlockSpec((1, H_SSM, P_SSM, N_STATE), lambda i, c: (i, 0, 0, 0)),
                  _const_spec(cw.shape), _const_spec(cb.shape), _const_spec(dtb.shape), _const_spec(alog.shape),
                  _const_spec(dskip.shape), _const_spec(gn.shape)],
        out_specs=(seq(D_SSM), pl.BlockSpec((1, H_SSM, P_SSM, N_STATE), lambda i, c: (i, 0, 0, 0))),
        out_shape=(jax.ShapeDtypeStruct((b, lp, D_SSM), BF16),
                   jax.ShapeDtypeStruct((b, H_SSM, P_SSM, N_STATE), F32)),
        scratch_shapes=[pltpu.VMEM((SUBLANES + lc, CONV_DIM), F32), pltpu.VMEM((H_SSM, P_SSM, N_STATE), F32)],
        compiler_params=_params("parallel", "arbitrary"),
        name="ssd",
    )(xbc, z, dt, cbuf, s0, cw, cb, dtb, alog, dskip, gn)


def _softmax_tile(q, k, v, bias, m_ref, l_ref, acc_ref):
    s = _dot_nt(q, k)
    if bias is not None:
        s = s + bias
    m_old = m_ref[...]
    m_new = jnp.maximum(m_old, jnp.max(s, axis=-1, keepdims=True))
    alpha = jnp.exp(m_old - m_new)
    p = jnp.exp(s - m_new)
    l_ref[...] = alpha * l_ref[...] + jnp.sum(p, axis=-1, keepdims=True)
    acc_ref[...] = alpha * acc_ref[...] + _dot(p.astype(BF16), v)
    m_ref[...] = m_new


def _softmax_init(m_ref, l_ref, acc_ref):
    m_ref[...] = jnp.full(m_ref.shape, NEG_INF, F32)
    l_ref[...] = jnp.zeros(l_ref.shape, F32)
    acc_ref[...] = jnp.zeros(acc_ref.shape, F32)


def _attn_kernel(lam_ref, qa_ref, qb_ref, k_ref, vt_ref, band_ref, gd_ref, o_ref, m_ref, l_ref, acc_ref, *, tq):
    i = pl.program_id(2)
    qt = jnp.concatenate([qa_ref[0, 0], qb_ref[0, 0]], axis=1)
    _softmax_init(m_ref, l_ref, acc_ref)

    def tile(j, bias):
        s = _dot(k_ref[0, j], qt)
        if bias is not None:
            s = s + jnp.concatenate([bias, bias], axis=1)
        m_old = m_ref[...]
        m_new = jnp.maximum(m_old, jnp.max(s, axis=0, keepdims=True))
        alpha = jnp.exp(m_old - m_new)
        p = jnp.exp(s - m_new)
        l_ref[...] = alpha * l_ref[...] + jnp.sum(p, axis=0, keepdims=True)
        acc_ref[...] = alpha * acc_ref[...] + _dot(vt_ref[0, j], p.astype(BF16))
        m_ref[...] = m_new

    def far(j, carry):
        tile(j, None)
        return carry

    lax.fori_loop(0, jnp.maximum(i - 1, 0), far, 0)

    @pl.when(i >= 1)
    def _():
        tile(i - 1, band_ref[0, 1])

    tile(i, band_ref[0, 0])

    o = acc_ref[...] / l_ref[...]
    o = o[:, :tq] - lam_ref[0] * o[:, tq:]
    o = o * lax.rsqrt(jnp.mean(o * o, axis=0, keepdims=True) + RMS_EPS) * gd_ref[...] * (1.0 - LAM_INIT)
    o_ref[0] = o.T.astype(BF16)


def prompt_attention(lam, qa, qb, kb, vt, band_t, gd, batch, seq, tq):
    nq = seq // tq
    head = 2 * DH
    k4 = kb.reshape(batch, nq, tq, D_DIFF)
    q_spec = pl.BlockSpec((1, 1, head, tq), lambda b, h, i: (b, i, h, 0))
    return pl.pallas_call(
        functools.partial(_attn_kernel, tq=tq),
        grid=(batch, H_DIFF, nq),
        in_specs=[pl.BlockSpec(memory_space=pltpu.SMEM), q_spec, q_spec,
                  pl.BlockSpec((1, nq, tq, head), lambda b, h, i: (b, 0, 0, h)),
                  pl.BlockSpec((1, nq, head, tq), lambda b, h, i: (b, 0, h, 0)),
                  pl.BlockSpec((1, 2, tq, tq), lambda b, h, i: (h, 0, 0, 0)),
                  _const_spec((head, 1))],
        out_specs=pl.BlockSpec((1, tq, head), lambda b, h, i: (b, i, h)),
        out_shape=jax.ShapeDtypeStruct((batch, seq, D_DIFF), BF16),
        scratch_shapes=[pltpu.VMEM((1, 2 * tq), F32), pltpu.VMEM((1, 2 * tq), F32), pltpu.VMEM((head, 2 * tq), F32)],
        compiler_params=_params("parallel", "parallel", "arbitrary"),
        name="prompt_attention",
    )(lam, qa, qb, k4, vt, band_t, gd.reshape(head, 1))


DEC_ROWS = 2 * H_DIFF


def _dec_attn_kernel(pt_ref, lam_ref, q_ref, *refs, n_groups, n_q, group):
    k_refs, v_refs = refs[:group], refs[group:2 * group]
    kn_ref, vn_ref, band_ref, expand_ref, gd_ref, o_ref, m_ref, l_ref, acc_ref = refs[2 * group:]
    j = pl.program_id(1)
    q = q_ref[0]
    page = band_ref.shape[2]

    @pl.when(j == 0)
    def _():
        _softmax_init(m_ref, l_ref, acc_ref)

    def update(s, weighted_values):
        m_old = m_ref[...]
        m_new = jnp.maximum(m_old, jnp.max(s, axis=-1, keepdims=True))
        alpha = jnp.exp(m_old - m_new)
        p = jnp.exp(s - m_new)
        l_ref[...] = alpha * l_ref[...] + jnp.sum(p, axis=-1, keepdims=True)
        acc_ref[...] = alpha * acc_ref[...] + weighted_values(p.astype(BF16))
        m_ref[...] = m_new

    scores = [_dot(q, k_ref[0].reshape(D_DIFF, page).astype(BF16)) for k_ref in k_refs]
    scores[-1] = scores[-1] + jnp.where(j == n_groups - 1, band_ref[0], 0.0)

    rows = q.shape[0]
    row_head = (lax.broadcasted_iota(I32, (rows, page * H_DIFF), 0) % (H_DIFF * n_q)) // n_q
    own_rows = (lax.broadcasted_iota(I32, (rows, page * H_DIFF), 1) % H_DIFF) == row_head

    def cached_values(p):
        o = None
        for u, v_ref in enumerate(v_refs):
            spread = _dot(p[:, u * page:(u + 1) * page], expand_ref[...])
            part = _dot(jnp.where(own_rows, spread, 0.0).astype(BF16), v_ref[0].astype(BF16))
            o = part if o is None else o + part
        return o

    update(jnp.concatenate(scores, axis=1), cached_values)

    @pl.when(j == n_groups - 1)
    def _():
        def new_values(p):
            full = _dot(p, vn_ref[0])
            head = (lax.broadcasted_iota(I32, (rows, 2 * DH), 0) % (H_DIFF * n_q)) // n_q
            return sum(jnp.where(head == h, full[:, h * 2 * DH:(h + 1) * 2 * DH], 0.0) for h in range(H_DIFF))

        update(_dot_nt(q, kn_ref[0]) + band_ref[1], new_values)
        o = acc_ref[...] / l_ref[...]
        half = rows // 2
        o = o[:half] - lam_ref[0] * o[half:]
        o = _rms(o, gd_ref[...]) * (1.0 - LAM_INIT)
        o_ref[0] = jnp.concatenate([o[h * n_q:(h + 1) * n_q] for h in range(H_DIFF)], axis=1).astype(BF16)


PAGE_GROUP = 8


def sample_attention(page_table, lam, qst, cache_kt, cache_v, kn, vn, band, gd, n_q):
    b, n_pages = page_table.shape
    page = cache_kt.shape[-1]
    rows = n_q * DEC_ROWS
    expand = jnp.repeat(jnp.eye(page, dtype=BF16), H_DIFF, axis=1)
    group = math.gcd(PAGE_GROUP, n_pages)
    n_groups = n_pages // group

    def page_of(u):
        return lambda i, j, pt: (pt[i, j * group + u],)

    k_specs = [pl.BlockSpec((1,) + cache_kt.shape[1:], lambda i, j, pt, f=page_of(u): f(i, j, pt) + (0, 0, 0, 0))
               for u in range(group)]
    v_specs = [pl.BlockSpec((1,) + cache_v.shape[1:], lambda i, j, pt, f=page_of(u): f(i, j, pt) + (0, 0))
               for u in range(group)]
    new_spec = pl.BlockSpec((1, page, D_DIFF), lambda i, j, pt: (i, 0, 0))
    grid_spec = pltpu.PrefetchScalarGridSpec(
        num_scalar_prefetch=1,
        grid=(b, n_groups),
        in_specs=[pl.BlockSpec(memory_space=pltpu.SMEM),
                  pl.BlockSpec((1, rows, D_DIFF), lambda i, j, pt: (i, 0, 0))]
                 + k_specs + v_specs
                 + [new_spec, new_spec,
                    pl.BlockSpec((2, rows, page), lambda i, j, pt: (0, 0, 0)),
                    pl.BlockSpec(expand.shape, lambda i, j, pt: (0, 0)),
                    pl.BlockSpec((1, 2 * DH), lambda i, j, pt: (0, 0))],
        out_specs=pl.BlockSpec((1, n_q, D_DIFF), lambda i, j, pt: (i, 0, 0)),
        scratch_shapes=[pltpu.VMEM((rows, 1), F32), pltpu.VMEM((rows, 1), F32), pltpu.VMEM((rows, 2 * DH), F32)],
    )
    return pl.pallas_call(
        functools.partial(_dec_attn_kernel, n_groups=n_groups, n_q=n_q, group=group),
        grid_spec=grid_spec,
        out_shape=jax.ShapeDtypeStruct((b, n_q, D_DIFF), BF16),
        compiler_params=_params("parallel", "arbitrary"),
        name="sample_attention",
    )(page_table, lam, qst, *([cache_kt] * group), *([cache_v] * group), kn, vn, band, expand, gd)


def _mid_kernel(x_ref, ys_ref, yd_ref, wa_ref, wb_ref, g_ref, wq_ref, h_ref, m_ref, qp_ref):
    h = x_ref[...] + _dot(ys_ref[...], wa_ref[...]) + _dot(yd_ref[...], wb_ref[...])
    h_ref[...] = h
    m = _rms(h, g_ref[...]).astype(BF16)
    mf = m.astype(F32)
    for c in range(D_MODEL // LANES):
        m_ref[:, c, :] = mf[:, c * LANES:(c + 1) * LANES]
    qp_ref[...] = _dot(m, wq_ref[...]).astype(BF16)


def mid(x, ys, yd, wa, wb, g, wq):
    t = x.shape[0]
    tm = min(ROW_TILE, t)
    row = lambda n: pl.BlockSpec((tm, n), lambda i: (i, 0))
    nq = wq.shape[1]
    return pl.pallas_call(
        _mid_kernel,
        grid=(t // tm,),
        in_specs=[row(D_MODEL), row(D_SSM), row(D_DIFF), _const_spec(wa.shape), _const_spec(wb.shape),
                  _const_spec((1, D_MODEL)), _const_spec(wq.shape)],
        out_specs=(row(D_MODEL), pl.BlockSpec((tm, D_MODEL // LANES, LANES), lambda i: (i, 0, 0)), row(nq)),
        out_shape=(jax.ShapeDtypeStruct((t, D_MODEL), F32), jax.ShapeDtypeStruct((t, D_MODEL // LANES, LANES), F32),
                   jax.ShapeDtypeStruct((t, nq), BF16)),
        compiler_params=_params("parallel"),
        name="mid",
    )(x, ys, yd, wa, wb, g, wq)


def _top16(vals):
    n = vals.shape[0]
    half = n // 2
    from_end = (n - 1 - lax.broadcasted_iota(I32, vals.shape, 0)).astype(F32)
    a, b = vals[:half], vals[half:]
    ia, ib = from_end[:half], from_end[half:]
    a_wins = a >= b
    win, win_i = jnp.where(a_wins, a, b), jnp.where(a_wins, ia, ib)
    lose, lose_i = jnp.where(a_wins, b, a), jnp.where(a_wins, ib, ia)
    out_v, out_i = [], []
    for _ in range(PEER_TOPK):
        mx = jnp.max(win, axis=0, keepdims=True)
        at = jnp.max(jnp.where(win == mx, win_i, -1.0), axis=0, keepdims=True)
        out_v.append(mx)
        out_i.append(at)
        taken = win_i == at
        win, win_i, lose = jnp.where(taken, lose, win), jnp.where(taken, lose_i, win_i), jnp.where(taken, -jnp.inf, lose)
    return jnp.concatenate(out_v, axis=0), n - 1 - jnp.concatenate(out_i, axis=0).astype(I32)


HI_MASK = -65536
LOW_HALF_SHIFT = 16


def _topk_kernel(qp_ref, sk_ref, row_ref, shift_ref, shift_t_ref, gate_ref):
    tb = qp_ref.shape[0]
    half = D_KEY // 2
    idx_rows, gate_rows = [], []
    for h in range(PEER_HEADS):
        s, ix = [], []
        for c in range(2):
            qh = qp_ref[:, (2 * h + c) * half:(2 * h + c + 1) * half]
            sc = _dot_nt(sk_ref[c], qh)
            v, i = _top16(sc)
            s.append(v)
            ix.append(i)
        pieces = [(a, PEER_TOPK if a == 0 else SUBLANES) for a in range(SUBLANES)]
        cand = jnp.concatenate([s[0][a:a + 1, :] + s[1][:nb] for a, nb in pieces] + [s[0][SUBLANES:] + s[1][0:1]],
                               axis=0)
        cidx = jnp.concatenate([ix[0][a:a + 1, :] * N_KEYS + ix[1][:nb] for a, nb in pieces]
                               + [ix[0][SUBLANES:] * N_KEYS + ix[1][0:1]], axis=0)
        sc, pos = _top16(cand)
        prow = lax.broadcasted_iota(I32, cand.shape, 0)
        cidx_f = cidx.astype(F32)
        idx = jnp.concatenate(
            [jnp.max(jnp.where(prow == pos[k:k + 1, :], cidx_f, -1.0), axis=0, keepdims=True)
             for k in range(PEER_TOPK)], axis=0).astype(I32)
        e = jnp.exp(sc - sc[0:1, :])
        gate_rows.append(e / jnp.sum(e, axis=0, keepdims=True))
        idx_rows.append(idx)
    idx_t = jnp.concatenate(idx_rows, axis=0)
    shift_t = ((idx_t & 1) ^ 1) * LOW_HALF_SHIFT
    shift_t_ref[0] = shift_t.astype(F32)
    idx = idx_t.T
    gate_ref[...] = jnp.concatenate(gate_rows, axis=0).T
    row_ref[...] = (idx >> 1) * SUBLANES
    shift_ref[...] = shift_t.T


def peer_topk(qp, sk):
    t = qp.shape[0]
    tb = min(PEER_TILE, t)
    out = pl.BlockSpec((tb, SLOTS), lambda i: (i, 0))
    return pl.pallas_call(
        _topk_kernel,
        grid=(t // tb,),
        in_specs=[pl.BlockSpec((tb, qp.shape[1]), lambda i: (i, 0)), _const_spec(sk.shape)],
        out_specs=(out, out, pl.BlockSpec((1, SLOTS, tb), lambda i: (i, 0, 0)), out),
        out_shape=(jax.ShapeDtypeStruct((t, SLOTS), I32), jax.ShapeDtypeStruct((t, SLOTS), I32),
                   jax.ShapeDtypeStruct((t // tb, SLOTS, tb), F32), jax.ShapeDtypeStruct((t, SLOTS), F32)),
        compiler_params=_params("parallel"),
        name="peer_topk",
    )(qp, sk)


def _pack_kernel(*refs):
    w_refs, o_ref = refs[:-1], refs[-1]
    chunks = len(w_refs)
    n = o_ref.shape[0] // chunks
    for c, w_ref in enumerate(w_refs):
        even = w_ref[pl.ds(0, n, stride=2), :].astype(BF16).astype(F32)
        odd = w_ref[pl.ds(1, n, stride=2), :].astype(BF16).astype(F32)
        lo = (lax.bitcast_convert_type(even, I32) >> 16) & 0xFFFF
        o_ref[pl.ds(c, n, stride=chunks), :] = lo | (lax.bitcast_convert_type(odd, I32) & HI_MASK)


def pack_table(w):
    n, d = w.shape
    rows = 512
    chunks = d // LANES
    return pl.pallas_call(
        _pack_kernel,
        grid=(n // (2 * rows),),
        in_specs=[pl.BlockSpec((2 * rows, LANES), lambda i, c=c: (i, c)) for c in range(chunks)],
        out_specs=pl.BlockSpec((rows * chunks, LANES), lambda i: (i, 0)),
        out_shape=jax.ShapeDtypeStruct((n // 2 * chunks, LANES), I32),
        compiler_params=_params("parallel"),
        name="pack_table",
    )(*([w] * chunks))


def _expert_row(tab_ref, row, shift):
    word = tab_ref[pl.ds(pl.multiple_of(row, SUBLANES), SUBLANES), :]
    return lax.bitcast_convert_type((word << shift) & HI_MASK, F32)


UNIT = SUBLANES


def _butterfly(parts, add, roll, select):
    def node(first, count, offset):
        if count == 1:
            return parts[first]
        run = SUBLANES // count
        half = count // 2
        o1 = offset % (2 * run)
        a = node(first, half, o1)
        b = node(first + half, half, (o1 + run) % (2 * run))
        takes_a = lambda p: ((p - o1) % (2 * run)) >= run
        return add(select(takes_a, a, b), roll(select(takes_a, b, a), run))

    return node(0, SUBLANES, 0)


def _butterfly_order():
    tiles = [[{p}] * SUBLANES for p in range(SUBLANES)]
    out = _butterfly(
        tiles,
        add=lambda a, b: [x | y for x, y in zip(a, b)],
        roll=lambda a, k: [a[(i - k) % SUBLANES] for i in range(SUBLANES)],
        select=lambda takes_a, a, b: [a[i] if takes_a(i) else b[i] for i in range(SUBLANES)])
    order = [0] * SUBLANES
    for sublane, owners in enumerate(out):
        (p,) = owners
        order[p] = sublane
    return order


BUTTERFLY_ORDER = _butterfly_order()


def _rows_to_sublanes(parts):
    sub = lax.broadcasted_iota(I32, (SUBLANES, LANES), 0)
    placed = [None] * SUBLANES
    for i, part in enumerate(parts):
        placed[BUTTERFLY_ORDER.index(i)] = part
    return _butterfly(placed, add=lambda a, b: a + b, roll=lambda a, k: pltpu.roll(a, k, 0),
                      select=lambda takes_a, a, b: jnp.where(takes_a(sub), a, b))


def _splat(ref, j, t):
    return jnp.broadcast_to(ref[j:j + 1, t * LANES:(t + 1) * LANES], (SUBLANES, LANES))


def _peer_u_kernel(row_ref, m_ref, gate_ref, shiftv_ref, shift_t_ref, spread_ref, tab_ref, wt_ref,
                   part_ref, h_ref, pad_ref, shift_splat_ref):
    tb = m_ref.shape[0]
    step = pl.program_id(0)

    @pl.when(step == 0)
    def _():
        pad_ref[...] = jnp.zeros(pad_ref.shape, F32)

    picks = spread_ref[step % spread_ref.shape[0]]
    shift_splat_ref[...] = _dot(shift_t_ref[0].astype(BF16), picks).astype(I32)

    def token(t):
        x = m_ref[t]
        for u in range(SLOTS // UNIT):
            parts = [_expert_row(tab_ref, row_ref[t, j], _splat(shift_splat_ref, j, t)) * x
                     for j in range(u * UNIT, (u + 1) * UNIT)]
            part_ref[t, u * UNIT:(u + 1) * UNIT, :] = _rows_to_sublanes(parts)

    for t in range(tb):
        token(t)
    for t in range(tb):
        h_ref[t:t + 1, :] = jnp.sum(part_ref[t].T, axis=0, keepdims=True)
    h = h_ref[...]
    act = 0.5 * h * (1.0 + lax.erf(h * (2.0 ** -0.5)))
    pad_ref[0:tb, :] = (gate_ref[...] * act).astype(BF16).astype(F32)
    pad_ref[tb:2 * tb, :] = shiftv_ref[...].astype(F32)
    wt_ref[0] = pad_ref[...].T


def _token_picks(n_tokens, tb):
    eye = jnp.eye(n_tokens, dtype=BF16).reshape(n_tokens, n_tokens // tb, tb)
    return jnp.repeat(jnp.moveaxis(eye, 1, 0), LANES, axis=2)


def peer_u(row, shift, shift_t, m3, gate, tab):
    t = m3.shape[0]
    tb = min(EXPERT_TILE, t)
    top_tb = shift_t.shape[2]
    per_top = top_tb // tb
    picks = _token_picks(top_tb, tb)
    smem = pl.BlockSpec((tb, SLOTS), lambda i: (i, 0), memory_space=pltpu.SMEM)
    vmem = pl.BlockSpec((tb, SLOTS), lambda i: (i, 0))
    return pl.pallas_call(
        _peer_u_kernel,
        grid=(t // tb,),
        in_specs=[smem, pl.BlockSpec((tb, SUBLANES, LANES), lambda i: (i, 0, 0)), vmem, vmem,
                  pl.BlockSpec((1, SLOTS, top_tb), lambda i: (i // per_top, 0, 0)), _const_spec(picks.shape),
                  pl.BlockSpec(tab.shape, lambda i: (0, 0), pipeline_mode=pl.Buffered(1))],
        out_specs=pl.BlockSpec((1, SLOTS, LANES), lambda i: (i, 0, 0)),
        out_shape=jax.ShapeDtypeStruct((t // tb, SLOTS, LANES), F32),
        scratch_shapes=[pltpu.VMEM((tb, SLOTS, LANES), F32), pltpu.VMEM((tb, SLOTS), F32),
                        pltpu.VMEM((LANES, SLOTS), F32), pltpu.VMEM((SLOTS, tb * LANES), I32)],
        compiler_params=_params("arbitrary"),
        name="peer_u",
    )(row, m3, gate, shift, shift_t, picks, tab)


N_ACC = 8


REGION_TOKENS = 4


def _token_regions(n, body):
    always = pl.program_id(0) >= 0

    def region(first):
        for t in range(first, min(first + REGION_TOKENS, n)):
            body(t)

    for first in range(0, n, REGION_TOKENS):
        pl.when(always)(functools.partial(region, first))


def _peer_v_kernel(row_ref, wt_ref, spread_ref, tab_ref, o_ref, code_ref):
    tb = o_ref.shape[0]
    spread = _dot(wt_ref[0][:, :2 * tb].astype(BF16), spread_ref[...])
    code_ref[...] = (lax.bitcast_convert_type(spread[:, :tb * LANES], I32)
                     | spread[:, tb * LANES:].astype(I32))

    def token(t):
        acc = [None] * N_ACC
        for j in range(SLOTS):
            code = _splat(code_ref, j, t)
            weight = lax.bitcast_convert_type(code & HI_MASK, F32)
            term = weight * _expert_row(tab_ref, row_ref[t, j], code & LOW_HALF_SHIFT)
            acc[j % N_ACC] = term if acc[j % N_ACC] is None else acc[j % N_ACC] + term
        while len(acc) > 1:
            acc = [a + b for a, b in zip(acc[0::2], acc[1::2])]
        o_ref[t] = acc[0]

    _token_regions(tb, token)


def peer_v(row, wt, tab):
    t = row.shape[0]
    tb = min(EXPERT_TILE, t)
    smem = pl.BlockSpec((tb, SLOTS), lambda i: (i, 0), memory_space=pltpu.SMEM)
    spread = jnp.repeat(jnp.eye(2 * tb, dtype=BF16), LANES, axis=1)
    return pl.pallas_call(
        _peer_v_kernel,
        grid=(t // tb,),
        in_specs=[smem, pl.BlockSpec((1, SLOTS, LANES), lambda i: (i, 0, 0)), _const_spec(spread.shape),
                  pl.BlockSpec(tab.shape, lambda i: (0, 0), pipeline_mode=pl.Buffered(1))],
        out_specs=pl.BlockSpec((tb, SUBLANES, LANES), lambda i: (i, 0, 0)),
        out_shape=jax.ShapeDtypeStruct((t, SUBLANES, LANES), F32),
        scratch_shapes=[pltpu.VMEM((SLOTS, tb * LANES), I32)],
        compiler_params=_params("arbitrary"),
        name="peer_v",
    )(row, wt, spread, tab)


def _ple_kernel(h_ref, peer_ref, p_ref, wp_ref, wg_ref, gp_ref, gf_ref, y_ref):
    peer = jnp.concatenate([peer_ref[:, c, :] for c in range(D_MODEL // LANES)], axis=1)
    h = h_ref[...] + peer
    gate = jax.nn.sigmoid(_dot(_rms(h, gp_ref[...]).astype(BF16), wg_ref[...]))
    h = h + _dot(p_ref[...].astype(BF16), wp_ref[...]) * gate
    y_ref[...] = _rms(h, gf_ref[...])


def ple(h, peer_out, p, wp, wg, gp, gf):
    t = h.shape[0]
    tm = min(ROW_TILE, t)
    row = lambda n: pl.BlockSpec((tm, n), lambda i: (i, 0))
    return pl.pallas_call(
        _ple_kernel,
        grid=(t // tm,),
        in_specs=[row(D_MODEL), pl.BlockSpec((tm, D_MODEL // LANES, LANES), lambda i: (i, 0, 0)), row(PLE_DIM),
                  _const_spec(wp.shape), _const_spec(wg.shape),
                  _const_spec((1, D_MODEL)), _const_spec((1, D_MODEL))],
        out_specs=row(D_MODEL),
        out_shape=jax.ShapeDtypeStruct((t, D_MODEL), F32),
        compiler_params=_params("parallel"),
        name="ple",
    )(h, peer_out, p, wp, wg, gp, gf)


def _t5_bucket(rel):
    n = jnp.maximum(rel, 0)
    max_exact = N_BUCKETS // 2
    nf = jnp.maximum(n, 1).astype(F32)
    large = max_exact + (jnp.log(nf / max_exact) / math.log(REL_MAX_DIST / max_exact)
                         * (N_BUCKETS - max_exact)).astype(I32)
    large = jnp.minimum(large, N_BUCKETS - 1)
    return jnp.where(n < max_exact, n, large)


def _band_bias(rel_bias, rel):
    far = rel_bias[N_BUCKETS - 1]
    bucket = _t5_bucket(rel)[..., None]
    b = jnp.zeros(rel.shape + (H_DIFF,), F32)
    for n in range(N_BUCKETS - 1):
        b = jnp.where(bucket == n, rel_bias[n] - far, b)
    b = jnp.where((rel >= 0)[..., None], b, NEG_INF)
    return jnp.moveaxis(b, -1, 0).astype(F32)


def _pad_rows(a, rows):
    return jnp.pad(a, ((0, 0), (0, rows - a.shape[1]), (0, 0)))


def _layer(x, p_l, conv_buf, ssm0, lam, w, attention, attn_tile):
    b, l, _ = x.shape
    t = b * l
    x2 = x.reshape(t, D_MODEL)
    z, xbc, q2, k, v, kb, vb, dt, *transposed = in_proj(x2, w["g_mix"], w["wz"], w["wx"], w["wq"], w["wk"], w["wv"],
                                                        w["wdt"], l, attn_tile)
    if k.ndim == 3:
        k = jnp.transpose(k.reshape(b, H_DIFF, 2, DH, l), (0, 4, 1, 2, 3))
    else:
        k = k.reshape(b, l, H_DIFF, 2, DH)

    lp = -(-l // SSD_CHUNK) * SSD_CHUNK
    xbc3 = xbc.reshape(b, l, CONV_DIM)
    cbuf = jnp.pad(conv_buf, ((0, 0), (SUBLANES - (CONV_W - 1), 0), (0, 0)))
    y_ssm, ssm_final = ssd(_pad_rows(xbc3, lp), _pad_rows(z.reshape(b, l, D_SSM), lp),
                           _pad_rows(dt.reshape(b, l, LANES), lp), cbuf, ssm0,
                           w["conv_w"], w["conv_b"], w["dt_bias"], w["a_log"], w["d_skip"], w["g_ssm_norm"], l)
    y_ssm = y_ssm[:, :l].reshape(t, D_SSM)
    tail = CONV_W - 1
    conv_new = xbc3[:, l - tail:] if l >= tail else jnp.concatenate([conv_buf, xbc3], axis=1)[:, -tail:]

    y_diff = attention(q2, kb, vb, transposed)

    h1, m, qp = mid(x2, y_ssm, y_diff, w["wo_a"], w["wo_b"], w["g_ffn"], w["w_query"])
    row, shift, shift_t, gate = peer_topk(qp, w["sub_keys"])
    wt = peer_u(row, shift, shift_t, m, gate, w["tab_u"])
    peer_out = peer_v(row, wt, w["tab_v"])
    y = ple(h1, peer_out, p_l.reshape(t, PLE_DIM), w["w_ple"], w["w_ple_gate"], w["g_ple"], w["g_final"])
    return (y.reshape(b, l, D_MODEL), k, v.reshape(b, l, H_DIFF, 2 * DH),
            ssm_final, conv_new)


def kernel(x_prompt, x_sample, cache_k, cache_v, state_ssm, state_conv, page_table, p_prompt, p_sample, g_mix, w_in, conv_w, conv_b, dt_bias, a_log, d_skip, g_ssm_norm, lam_q1, lam_k1, lam_q2, lam_k2, g_diff_norm, rel_bias, w_out, g_ffn, w_query, sub_keys, expert_u, expert_v, w_ple, w_ple_gate, g_ple, g_final):
    assert w_in.shape[0] == 1, "single-layer problem"
    bsz, seq, _ = x_prompt.shape
    dec_b, dec_l, _ = x_sample.shape
    n_pages = page_table.shape[1]
    page = cache_k.shape[2]
    past = n_pages * page

    wi = w_in[0].astype(BF16)
    o_x = D_SSM
    o_dt = o_x + CONV_DIM
    o_q = o_dt + H_SSM
    o_k = o_q + D_DIFF
    o_v = o_k + D_DIFF
    lanes_of = lambda a: jnp.pad(a.astype(F32)[None, :], ((0, 0), (0, LANES - a.shape[0])))
    w = dict(
        g_mix=g_mix[0][None], wz=wi[:, :o_x], wx=wi[:, o_x:o_dt],
        wdt=jnp.pad(wi[:, o_dt:o_q], ((0, 0), (0, LANES - H_SSM))),
        wq=wi[:, o_q:o_k], wk=wi[:, o_k:o_v], wv=wi[:, o_v:],
        conv_w=conv_w[0], conv_b=conv_b[0][None], dt_bias=lanes_of(dt_bias[0]), a_log=lanes_of(a_log[0]),
        d_skip=lanes_of(d_skip[0]), g_ssm_norm=g_ssm_norm[0][None],
        wo_a=w_out[0][:D_SSM].astype(BF16), wo_b=w_out[0][D_SSM:].astype(BF16), g_ffn=g_ffn[0][None],
        w_query=w_query[0].astype(BF16), sub_keys=sub_keys[0].astype(BF16),
        tab_u=pack_table(expert_u[0]), tab_v=pack_table(expert_v[0]),
        w_ple=w_ple[0].astype(BF16), w_ple_gate=w_ple_gate[0].astype(BF16), g_ple=g_ple[0][None],
        g_final=g_final[None],
    )
    gd = g_diff_norm[0][None]
    lam = (jnp.exp(jnp.sum(lam_q1[0] * lam_k1[0])) - jnp.exp(jnp.sum(lam_q2[0] * lam_k2[0])) + LAM_INIT).reshape(1)

    tq = min(ATTN_TILE, seq)
    ii = jnp.arange(tq, dtype=I32)
    assert tq >= REL_MAX_DIST and page >= REL_MAX_DIST, "bias is only applied next to the diagonal"
    rel = jnp.stack([ii[None, :] - ii[:, None], tq + ii[None, :] - ii[:, None]])
    band_p = _band_bias(rel_bias, rel)

    def attn_prompt(q2, kb, vb, transposed):
        if transposed:
            qa, qb, vt = transposed
        else:
            to_tiles = lambda a: jnp.swapaxes(a.reshape(bsz, seq // tq, tq, D_DIFF), 2, 3)
            qa, qb, vt = to_tiles(q2[0]), to_tiles(q2[1]), to_tiles(vb)
        return prompt_attention(lam, qa, qb, kb, vt, band_p, gd, bsz, seq, tq).reshape(bsz * seq, D_DIFF)

    yp, kp, vp, sp, cp = _layer(x_prompt, p_prompt[0], jnp.zeros((bsz, CONV_W - 1, CONV_DIM), F32),
                                jnp.zeros((bsz, H_SSM, P_SSM, N_STATE), F32), lam, w, attn_prompt, tq)

    rows = dec_l * DEC_ROWS
    r = jnp.arange(rows, dtype=I32)
    qpos = past + r % dec_l
    head_of = (r % (H_DIFF * dec_l)) // dec_l
    map_of = r // (H_DIFF * dec_l)
    kk = jnp.arange(page, dtype=I32)
    rel_last = qpos[:, None] - (past - page + kk)[None, :]
    rel_new = jnp.where(kk[None, :] < dec_l, qpos[:, None] - (past + kk)[None, :], -1)
    band_all = _band_bias(rel_bias, jnp.stack([rel_last, rel_new]))
    band_s = jnp.take_along_axis(band_all, head_of[None, None, :, None], axis=0)[0]
    lane_slot = jnp.arange(D_DIFF, dtype=I32) // DH
    q_mask = (lane_slot[None, :] == (head_of * 2 + map_of)[:, None])
    cache_v2 = cache_v[0].reshape(-1, page * H_DIFF, 2 * DH)

    def attn_sample(q2, kb, vb, transposed):
        q = (q2[0] + q2[1]).reshape(dec_b, dec_l, D_DIFF)
        qst = jnp.where(q_mask[None], jnp.tile(q, (1, DEC_ROWS, 1)), jnp.zeros((), BF16))
        kn = _pad_rows(kb.reshape(dec_b, dec_l, D_DIFF), page)
        vn = _pad_rows(vb.reshape(dec_b, dec_l, D_DIFF), page)
        o = sample_attention(page_table, lam, qst, jnp.transpose(cache_k[0], (0, 2, 3, 4, 1)), cache_v2,
                             kn, vn, band_s, gd, dec_l)
        return o.reshape(dec_b * dec_l, D_DIFF)

    ys, ks, vs, ss, cs = _layer(x_sample, p_sample[0], state_conv[0], state_ssm[0], lam, w, attn_sample, page)

    return (yp, ys, kp[None], vp[None], sp[None], cp[None], ks[None], vs[None], ss[None], cs[None])
```
